```python
import jax, jax.numpy as jnp
from jax import lax
import numpy as np

D_MODEL = 1024
BATCH = 8
SEQ = 2048
DEPTH = 1
DEC_BATCH = 128
DEC_SEQ = 8
PAST_LEN = 16384
PAGE_SIZE = 128

HGRN_HEADS = 4
HGRN_KEY = 128
HGRN_VAL = 128
HGRN_FDIM = HGRN_HEADS * HGRN_KEY
HGRN_WIDTH = HGRN_HEADS * HGRN_VAL
HGRN_CHUNK = 64
GMLP_GROUPS = 4
GMLP_CHUNK = 128
GMLP_WIDTH = 512
GMLP_GW = GMLP_WIDTH // GMLP_GROUPS
D_FF = 2816
CONV_W = 3
PLE_DIM = 256
N_IN = 2 * HGRN_FDIM + 2 * HGRN_WIDTH + 2 * GMLP_WIDTH + 2 * D_MODEL
EPS = 1e-6

kernel_name = "hgrn2_chunkmlp_convffn_hybrid_step"


def rmsnorm(x, w):
    xf = x.astype(jnp.float32)
    y = xf * lax.rsqrt(jnp.mean(xf * xf, axis=-1, keepdims=True) + EPS)
    return (y * w.astype(jnp.float32)).astype(x.dtype)


def layernorm(x, w, b):
    xf = x.astype(jnp.float32)
    xc = xf - jnp.mean(xf, axis=-1, keepdims=True)
    y = xc * lax.rsqrt(jnp.mean(xc * xc, axis=-1, keepdims=True) + EPS)
    return (y * w.astype(jnp.float32) + b.astype(jnp.float32)).astype(x.dtype)


def hgrn2_recurrence(q, k, v, log_f, s0):
    B, T, H, K = q.shape
    c = min(HGRN_CHUNK, T)
    n = T // c

    def chunks(a):
        return a.reshape(B, n, c, H, a.shape[-1]).transpose(1, 0, 3, 2, 4)

    causal = jnp.tril(jnp.ones((c, c), dtype=bool))[:, :, None]

    def step(S, blk):
        qb, kb, vb, gb = blk
        b = jnp.cumsum(gb, axis=2)
        o_inter = jnp.einsum("bhtk,bhkv->bhtv", qb * jnp.exp(b), S)
        rel = jnp.where(causal, b[:, :, :, None, :] - b[:, :, None, :, :], -jnp.inf)
        scores = jnp.einsum("bhtk,bhsk,bhtsk->bhts", qb, kb, jnp.exp(rel))
        o = o_inter + jnp.einsum("bhts,bhsv->bhtv", scores, vb)
        b_end = b[:, :, -1, :]
        S = jnp.exp(b_end)[..., None] * S + jnp.einsum(
            "bhsk,bhsv->bhkv", kb * jnp.exp(b_end[:, :, None, :] - b), vb)
        return S, o

    S, o = lax.scan(step, s0, (chunks(q), chunks(k), chunks(v), chunks(log_f)))
    return o.transpose(1, 0, 3, 2, 4).reshape(B, T, H * v.shape[-1]), S


def chunk_spatial_gate(u, v, w_s, b_s):
    B, T, _ = v.shape
    c = min(GMLP_CHUNK, T)
    n = T // c
    vg = v.reshape(B, n, c, GMLP_GROUPS, GMLP_GW)
    w = jnp.tril(w_s[:, :c, :c])
    mixed = jnp.einsum("gts,bnsgc->bntgc", w, vg) + b_s[:, :c].T[None, None, :, :, None]
    return u * mixed.reshape(B, T, GMLP_WIDTH)


def conv_ffn(xn, w_up, conv_w, conv_b, w_down, conv_state):
    T = xn.shape[1]
    h = xn @ w_up
    hp = jnp.concatenate([conv_state.astype(h.dtype), h], axis=1)
    y = conv_b + hp[:, 0:T] * conv_w[0]
    for j in range(1, CONV_W):
        y = y + hp[:, j:j + T] * conv_w[j]
    a, b = jnp.split(y, 2, axis=-1)
    return (jax.nn.gelu(a, approximate=False) * b) @ w_down, hp[:, T:]


def layer_forward(x, p, s_hgrn, s_conv, lb, norm_mix_w, w_in, hgrn_norm_w, ln_v_w, ln_v_b,
                  w_spatial, b_spatial, w_a_out, w_b_out, w_o, norm_ffn_w, w_up, conv_w,
                  conv_b, w_down, norm_ple_w, w_ple_gate, w_ple_proj):
    B, T, _ = x.shape
    f32 = jnp.float32
    xn = rmsnorm(x, norm_mix_w)
    z = xn @ w_in
    o1 = HGRN_FDIM
    o2 = o1 + HGRN_FDIM
    o3 = o2 + HGRN_WIDTH
    o4 = o3 + HGRN_WIDTH
    o5 = o4 + GMLP_WIDTH
    o6 = o5 + GMLP_WIDTH
    o7 = o6 + D_MODEL
    q_pre, f_pre, i_in, og, u_pre, v_pre, g_a, g_b = jnp.split(z, [o1, o2, o3, o4, o5, o6, o7], axis=-1)

    fg = lb + (1.0 - lb) * jax.nn.sigmoid(f_pre.astype(f32))
    o_rec, s_new = hgrn2_recurrence(
        jax.nn.silu(q_pre.astype(f32)).reshape(B, T, HGRN_HEADS, HGRN_KEY),
        (1.0 - fg).reshape(B, T, HGRN_HEADS, HGRN_KEY),
        i_in.astype(f32).reshape(B, T, HGRN_HEADS, HGRN_VAL),
        jnp.log(fg).reshape(B, T, HGRN_HEADS, HGRN_KEY),
        s_hgrn.astype(f32))
    o_a = rmsnorm(o_rec.astype(x.dtype) * jax.nn.sigmoid(og), hgrn_norm_w)

    u = jax.nn.gelu(u_pre, approximate=False)
    v = layernorm(jax.nn.gelu(v_pre, approximate=False), ln_v_w, ln_v_b)
    o_b = chunk_spatial_gate(u, v, w_spatial, b_spatial)

    mix = (jax.nn.sigmoid(g_a) * (o_a @ w_a_out) + jax.nn.sigmoid(g_b) * (o_b @ w_b_out)) @ w_o
    h = x + mix

    ff, conv_new = conv_ffn(rmsnorm(h, norm_ffn_w), w_up, conv_w, conv_b, w_down, s_conv)
    h = h + ff

    h = h + jax.nn.sigmoid(rmsnorm(h, norm_ple_w) @ w_ple_gate) * (p @ w_ple_proj)
    return h, s_new.astype(s_hgrn.dtype), conv_new, v


def setup_inputs(seed: int = 0) -> dict:
    key = jax.random.key(seed)
    ks = iter(jax.random.split(key, 32))

    def nrm(shape, scale):
        return jax.random.normal(next(ks), shape, jnp.float32) * scale

    L = DEPTH
    return {
        "x_prompt": nrm((BATCH, SEQ, D_MODEL), 1.0),
        "x_sample": nrm((DEC_BATCH, DEC_SEQ, D_MODEL), 1.0),
        "p_prompt": nrm((L, BATCH, SEQ, PLE_DIM), 1.0),
        "p_sample": nrm((L, DEC_BATCH, DEC_SEQ, PLE_DIM), 1.0),
        "state_hgrn": nrm((L, DEC_BATCH, HGRN_HEADS, HGRN_KEY, HGRN_VAL), 0.5),
        "state_conv": nrm((L, DEC_BATCH, CONV_W - 1, 2 * D_FF), 1.0),
        "lb_logits": nrm((L + 1, HGRN_FDIM), 0.5),
        "norm_mix_w": 1.0 + nrm((L, D_MODEL), 0.02),
        "w_in": nrm((L, D_MODEL, N_IN), D_MODEL ** -0.5),
        "hgrn_norm_w": 1.0 + nrm((L, HGRN_WIDTH), 0.02),
        "ln_v_w": 1.0 + nrm((L, GMLP_WIDTH), 0.02),
        "ln_v_b": nrm((L, GMLP_WIDTH), 0.02),
        "w_spatial": nrm((L, GMLP_GROUPS, GMLP_CHUNK, GMLP_CHUNK), GMLP_CHUNK ** -0.5),
        "b_spatial": 1.0 + nrm((L, GMLP_GROUPS, GMLP_CHUNK), 0.02),
        "w_a_out": nrm((L, HGRN_WIDTH, D_MODEL), HGRN_WIDTH ** -0.5),
        "w_b_out": nrm((L, GMLP_WIDTH, D_MODEL), GMLP_WIDTH ** -0.5),
        "w_o": nrm((L, D_MODEL, D_MODEL), D_MODEL ** -0.5),
        "norm_ffn_w": 1.0 + nrm((L, D_MODEL), 0.02),
        "w_up": nrm((L, D_MODEL, 2 * D_FF), D_MODEL ** -0.5),
        "conv_w": nrm((L, CONV_W, 2 * D_FF), CONV_W ** -0.5),
        "conv_b": nrm((L, 2 * D_FF), 0.02),
        "w_down": nrm((L, D_FF, D_MODEL), D_FF ** -0.5),
        "norm_ple_w": 1.0 + nrm((L, D_MODEL), 0.02),
        "w_ple_gate": nrm((L, D_MODEL, D_MODEL), D_MODEL ** -0.5),
        "w_ple_proj": nrm((L, PLE_DIM, D_MODEL), PLE_DIM ** -0.5),
        "final_norm_w": 1.0 + nrm((D_MODEL,), 0.02),
    }


def reference(x_prompt, x_sample, p_prompt, p_sample, state_hgrn, state_conv, lb_logits,
              norm_mix_w, w_in, hgrn_norm_w, ln_v_w, ln_v_b, w_spatial, b_spatial, w_a_out,
              w_b_out, w_o, norm_ffn_w, w_up, conv_w, conv_b, w_down, norm_ple_w, w_ple_gate,
              w_ple_proj, final_norm_w):
    lb_all = jnp.cumsum(jax.nn.softmax(lb_logits.astype(jnp.float32), axis=0), axis=0)
    hp, hs = x_prompt, x_sample
    hgrn_p, hgrn_s, conv_p, conv_s, v_s = [], [], [], [], []
    for l in range(DEPTH):
        lw = (norm_mix_w[l], w_in[l], hgrn_norm_w[l], ln_v_w[l], ln_v_b[l], w_spatial[l],
              b_spatial[l], w_a_out[l], w_b_out[l], w_o[l], norm_ffn_w[l], w_up[l], conv_w[l],
              conv_b[l], w_down[l], norm_ple_w[l], w_ple_gate[l], w_ple_proj[l])
        s0_p = jnp.zeros((BATCH, HGRN_HEADS, HGRN_KEY, HGRN_VAL), jnp.float32)
        c0_p = jnp.zeros((BATCH, CONV_W - 1, 2 * D_FF), hp.dtype)
        hp, sp, cp, _ = layer_forward(hp, p_prompt[l], s0_p, c0_p, lb_all[l], *lw)
        hs, ss, cs, vs = layer_forward(hs, p_sample[l], state_hgrn[l], state_conv[l], lb_all[l], *lw)
        hgrn_p.append(sp)
        hgrn_s.append(ss)
        conv_p.append(cp)
        conv_s.append(cs)
        v_s.append(vs)
    y_prompt = rmsnorm(hp, final_norm_w)
    y_sample = rmsnorm(hs, final_norm_w)
    hgrn_state_prompt = jnp.stack(hgrn_p, axis=0)
    hgrn_state_sample = jnp.stack(hgrn_s, axis=0)
    conv_state_prompt = jnp.stack(conv_p, axis=0)
    conv_state_sample = jnp.stack(conv_s, axis=0)
    gmlp_v_sample = jnp.stack(v_s, axis=0)
    return (y_prompt, y_sample, hgrn_state_prompt, hgrn_state_sample, conv_state_prompt, conv_state_sample, gmlp_v_sample)
```

```python
import functools

import jax
import jax.numpy as jnp
from jax import lax
from jax.experimental import pallas as pl
from jax.experimental.pallas import tpu as pltpu

F32 = jnp.float32
BF16 = jnp.bfloat16
EPS = 1e-6

HEADS = 4
GROUPS = 4
CONV_W = 3
LANES = 128
SUBLANES = 8
FF_TILE = 256
VMEM_LIMIT = 56 * 1024 * 1024

NT_DIMS = (((1,), (1,)), ((), ()))
TN_DIMS = (((0,), (0,)), ((), ()))


def _rms(x, w):
    return x * lax.rsqrt(jnp.mean(x * x, axis=-1, keepdims=True) + EPS) * w


def _gelu(x):
    return 0.5 * x * (1.0 + lax.erf(x * (2.0 ** -0.5)))


def _dot(a, b):
    return jnp.dot(a, b, preferred_element_type=F32)


def _hgrn_chunk(qh, kh, bh, vh, st):
    c = qh.shape[0]
    b_mid = bh[c // 2 - 1:c // 2, :]
    b_end = bh[c - 1:c, :]
    q_in = (qh * jnp.exp(bh)).astype(BF16)
    q_t = (qh * jnp.exp(bh - b_mid)).astype(BF16)
    k_t = (kh * jnp.exp(b_mid - bh)).astype(BF16)
    k_e = (kh * jnp.exp(b_end - bh)).astype(BF16)
    vb = vh.astype(BF16)
    row = lax.broadcasted_iota(jnp.int32, (c, c), 0)
    col = lax.broadcasted_iota(jnp.int32, (c, c), 1)
    scores = lax.dot_general(q_t, k_t, NT_DIMS, preferred_element_type=F32)
    scores = jnp.where(col <= row, scores, 0.0).astype(BF16)
    o = lax.dot_general(q_in, st.astype(BF16), NT_DIMS, preferred_element_type=F32)
    o = o + _dot(scores, vb)
    st_new = st * jnp.exp(b_end) + lax.dot_general(vb, k_e, TN_DIMS, preferred_element_type=F32)
    return o, st_new


def _mixer_kernel(*refs, sample, rows, chunk, n_t):
    if sample:
        (x_ref, lbl_ref, nmw_ref, win_ref, hnw_ref, lvw_ref, lvb_ref, wmix_ref, bcol_ref,
         wa_ref, wb_ref, wo_ref, sin_ref, h_ref, sout_ref, vout_ref,
         q_s, k_s, b_s, v_s, o_s) = refs
    else:
        (x_ref, lbl_ref, nmw_ref, win_ref, hnw_ref, lvw_ref, lvb_ref, wmix_ref, bcol_ref,
         wa_ref, wb_ref, wo_ref, h_ref, sout_ref,
         st_s, q_s, k_s, b_s, v_s, o_s) = refs
    fdim = HEADS * LANES
    gw = GROUPS * LANES
    d_model = x_ref.shape[1]
    offs = [0]
    for width in (fdim, fdim, fdim, fdim, gw, gw, d_model, d_model):
        offs.append(offs[-1] + width)

    x = x_ref[...]
    xn = _rms(x, nmw_ref[...]).astype(BF16)

    def proj(i):
        return _dot(xn, win_ref[:, offs[i]:offs[i + 1]])

    lbl = lbl_ref[...]
    lbe = jnp.exp(lbl - jnp.max(lbl, axis=0, keepdims=True))
    lb = lbe[0:1, :] / jnp.sum(lbe, axis=0, keepdims=True)

    q_s[...] = jax.nn.silu(proj(0))
    fg = lb + (1.0 - lb) * jax.nn.sigmoid(proj(1))
    k_s[...] = 1.0 - fg
    v_s[...] = proj(2)
    r128 = lax.broadcasted_iota(jnp.int32, (LANES, LANES), 0)
    c128 = lax.broadcasted_iota(jnp.int32, (LANES, LANES), 1)
    shift = chunk.bit_length() - 1
    seg = jnp.where(((r128 >> shift) == (c128 >> shift)) & (c128 <= r128), 1.0, 0.0).astype(F32)
    log_f = jnp.log(fg)
    for g in range(rows // LANES):
        b_s[g * LANES:(g + 1) * LANES, :] = jnp.dot(
            seg, log_f[g * LANES:(g + 1) * LANES, :], precision=lax.Precision.HIGHEST,
            preferred_element_type=F32)

    if not sample:
        @pl.when(pl.program_id(1) == 0)
        def _():
            st_s[...] = jnp.zeros_like(st_s)

    def chunk_step(j, carry):
        r0 = pl.multiple_of(j * chunk, chunk)
        for hd in range(HEADS):
            cols = slice(hd * LANES, (hd + 1) * LANES)
            st = sin_ref[j, hd].T if sample else st_s[hd]
            o, st_new = _hgrn_chunk(q_s[pl.ds(r0, chunk), cols], k_s[pl.ds(r0, chunk), cols],
                                    b_s[pl.ds(r0, chunk), cols], v_s[pl.ds(r0, chunk), cols], st)
            o_s[pl.ds(r0, chunk), cols] = o
            if sample:
                sout_ref[j, hd] = st_new.T
            else:
                st_s[hd] = st_new
        return carry

    lax.fori_loop(0, rows // chunk, chunk_step, 0)

    if not sample:
        @pl.when(pl.program_id(1) == n_t - 1)
        def _():
            for hd in range(HEADS):
                sout_ref[0, hd] = st_s[hd].T

    o_a = _rms(o_s[...] * jax.nn.sigmoid(proj(3)), hnw_ref[...]).astype(BF16)

    u = _gelu(proj(4))
    gv = _gelu(proj(5))
    gc = gv - jnp.mean(gv, axis=-1, keepdims=True)
    v = gc * lax.rsqrt(jnp.mean(gc * gc, axis=-1, keepdims=True) + EPS) * lvw_ref[...] + lvb_ref[...]
    if sample:
        vout_ref[...] = v
    vb = v.astype(BF16)
    wm = [jnp.where(c128 <= r128, wmix_ref[g], 0.0).astype(BF16) for g in range(GROUPS)]
    bcol = [bcol_ref[g] for g in range(GROUPS)]
    mixed_rows = []
    for rg in range(rows // LANES):
        rsl = slice(rg * LANES, (rg + 1) * LANES)
        mixed_rows.append(jnp.concatenate(
            [_dot(wm[g], vb[rsl, g * LANES:(g + 1) * LANES]) + bcol[g] for g in range(GROUPS)],
            axis=1))
    o_b = (u * jnp.concatenate(mixed_rows, axis=0)).astype(BF16)

    mix = (jax.nn.sigmoid(proj(6)) * _dot(o_a, wa_ref[...])
           + jax.nn.sigmoid(proj(7)) * _dot(o_b, wb_ref[...])).astype(BF16)
    h_ref[...] = x + _dot(mix, wo_ref[...])


def _ffn_kernel(*refs, sample, rows, seq):
    if sample:
        (h_ref, p_ref, nfw_ref, wup_ref, cw_ref, cb_ref, wdn_ref, npw_ref, wpg_ref, wpp_ref,
         fnw_ref, cst_ref, y_ref, cout_ref) = refs
    else:
        (h_ref, p_ref, nfw_ref, wup_ref, cw_ref, cb_ref, wdn_ref, npw_ref, wpg_ref, wpp_ref,
         fnw_ref, y_ref, cout_ref, carry_s) = refs
        @pl.when(pl.program_id(1) == 0)
        def _():
            carry_s[...] = jnp.zeros_like(carry_s)
    d_ff = wdn_ref.shape[0]
    n_seq = rows // seq
    h = h_ref[...]
    hn = _rms(h, nfw_ref[...]).astype(BF16)
    rid = lax.broadcasted_iota(jnp.int32, (rows, FF_TILE), 0)
    tpos = (rid & (seq - 1)) if sample else rid

    def conv(up, lo):
        cols = slice(lo, lo + FF_TILE)
        w = cw_ref[:, cols]
        if sample:
            st = cst_ref[:, :, cols]
            e0 = jnp.broadcast_to(st[:, 0:1, :], (n_seq, seq, FF_TILE)).reshape(rows, FF_TILE)
            e1 = jnp.broadcast_to(st[:, 1:2, :], (n_seq, seq, FF_TILE)).reshape(rows, FF_TILE)
            cout_ref[:, :, cols] = up.reshape(n_seq, seq, FF_TILE)[:, seq - 2:seq, :]
        else:
            e0 = carry_s[0:1, cols]
            e1 = carry_s[1:2, cols]
            carry_s[:, cols] = up[rows - 2:rows, :]
            cout_ref[0, :, cols] = up[rows - 2:rows, :]
        x1 = jnp.where(tpos >= 1, pltpu.roll(up, 1, 0), e1)
        x2 = jnp.where(tpos >= 2, pltpu.roll(up, 2, 0), jnp.where(tpos == 1, e1, e0))
        return cb_ref[:, cols] + x2 * w[0:1, :] + x1 * w[1:2, :] + up * w[2:3, :]

    acc = jnp.zeros((rows, h.shape[1]), F32)
    for j in range(d_ff // FF_TILE):
        lo_a = j * FF_TILE
        lo_b = d_ff + j * FF_TILE
        ya = conv(_dot(hn, wup_ref[:, lo_a:lo_a + FF_TILE]), lo_a)
        yb = conv(_dot(hn, wup_ref[:, lo_b:lo_b + FF_TILE]), lo_b)
        acc = acc + _dot((_gelu(ya) * yb).astype(BF16), wdn_ref[lo_a:lo_a + FF_TILE, :])
    h2 = h + acc
    gate = jax.nn.sigmoid(_dot(_rms(h2, npw_ref[...]).astype(BF16), wpg_ref[...]))
    h3 = h2 + gate * _dot(p_ref[...].astype(BF16), wpp_ref[...])
    y_ref[...] = _rms(h3, fnw_ref[...])


def _const_spec(shape):
    nd = len(shape)
    return pl.BlockSpec(shape, lambda *_: (0,) * nd, pipeline_mode=pl.Buffered(1))


def _mixer(x, state, lb_logits, weights, wmix, bcol, *, rows, chunk):
    batch, seq, d = x.shape
    sample = state is not None
    x2 = x.reshape(batch * seq, d)
    fdim = HEADS * LANES
    const_args = (lb_logits,) + tuple(weights[:5]) + (wmix, bcol) + tuple(weights[5:])
    const_specs = [_const_spec(a.shape) for a in const_args]
    scratch = [pltpu.VMEM((rows, fdim), F32) for _ in range(5)]
    if sample:
        n_seq = rows // seq
        grid = (batch // n_seq,)
        row_map = lambda i: (i, 0)
        st_spec = pl.BlockSpec((n_seq, HEADS, LANES, LANES), lambda i: (i, 0, 0, 0))
        in_specs = [pl.BlockSpec((rows, d), row_map)] + const_specs + [st_spec]
        out_specs = [pl.BlockSpec((rows, d), row_map), st_spec,
                     pl.BlockSpec((rows, GROUPS * LANES), row_map)]
        out_shape = [jax.ShapeDtypeStruct((batch * seq, d), F32),
                     jax.ShapeDtypeStruct(state.shape, F32),
                     jax.ShapeDtypeStruct((batch * seq, GROUPS * LANES), F32)]
        args = (x2,) + const_args + (state,)
        n_t = 1
        sem = ("arbitrary",)
    else:
        n_t = seq // rows
        grid = (batch, n_t)
        row_map = lambda b, t: (b * n_t + t, 0)
        in_specs = [pl.BlockSpec((rows, d), row_map)] + const_specs
        out_specs = [pl.BlockSpec((rows, d), row_map),
                     pl.BlockSpec((1, HEADS, LANES, LANES), lambda b, t: (b, 0, 0, 0))]
        out_shape = [jax.ShapeDtypeStruct((batch * seq, d), F32),
                     jax.ShapeDtypeStruct((batch, HEADS, LANES, LANES), F32)]
        scratch = [pltpu.VMEM((HEADS, LANES, LANES), F32)] + scratch
        args = (x2,) + const_args
        sem = ("arbitrary", "arbitrary")
    return pl.pallas_call(
        functools.partial(_mixer_kernel, sample=sample, rows=rows, chunk=chunk, n_t=n_t),
        grid=grid, in_specs=in_specs, out_specs=out_specs, out_shape=out_shape,
        scratch_shapes=scratch,
        compiler_params=pltpu.CompilerParams(dimension_semantics=sem,
                                             vmem_limit_bytes=VMEM_LIMIT),
        name="mixer_sample" if sample else "mixer_prompt",
    )(*args)


def _ffn(h2d, p2d, conv_state, weights, *, batch, seq, rows):
    d = h2d.shape[1]
    sample = conv_state is not None
    const_specs = [_const_spec(a.shape) for a in weights]
    width = weights[2].shape[1]
    if sample:
        n_seq = rows // seq
        grid = (batch // n_seq,)
        row_map = lambda i: (i, 0)
        cs_spec = pl.BlockSpec((n_seq, CONV_W - 1, width), lambda i: (i, 0, 0))
        in_specs = ([pl.BlockSpec((rows, d), row_map), pl.BlockSpec((rows, p2d.shape[1]), row_map)]
                    + const_specs + [cs_spec])
        out_specs = [pl.BlockSpec((rows, d), row_map), cs_spec]
        args = (h2d, p2d) + tuple(weights) + (conv_state,)
        scratch = []
        sem = ("arbitrary",)
    else:
        n_t = seq // rows
        grid = (batch, n_t)
        row_map = lambda b, t: (b * n_t + t, 0)
        in_specs = ([pl.BlockSpec((rows, d), row_map), pl.BlockSpec((rows, p2d.shape[1]), row_map)]
                    + const_specs)
        out_specs = [pl.BlockSpec((rows, d), row_map),
                     pl.BlockSpec((1, CONV_W - 1, width), lambda b, t: (b, 0, 0))]
        args = (h2d, p2d) + tuple(weights)
        scratch = [pltpu.VMEM((CONV_W - 1, width), F32)]
        sem = ("arbitrary", "arbitrary")
    out_shape = [jax.ShapeDtypeStruct((batch * seq, d), F32),
                 jax.ShapeDtypeStruct((batch, CONV_W - 1, width), F32)]
    return pl.pallas_call(
        functools.partial(_ffn_kernel, sample=sample, rows=rows, seq=seq),
        grid=grid, in_specs=in_specs, out_specs=out_specs, out_shape=out_shape,
        scratch_shapes=scratch,
        compiler_params=pltpu.CompilerParams(dimension_semantics=sem,
                                             vmem_limit_bytes=VMEM_LIMIT),
        name="ffn_sample" if sample else "ffn_prompt",
    )(*args)


def kernel(x_prompt, x_sample, p_prompt, p_sample, state_hgrn, state_conv, lb_logits, norm_mix_w, w_in, hgrn_norm_w, ln_v_w, ln_v_b, w_spatial, b_spatial, w_a_out, w_b_out, w_o, norm_ffn_w, w_up, conv_w, conv_b, w_down, norm_ple_w, w_ple_gate, w_ple_proj, final_norm_w):
    depth = w_in.shape[0]
    assert depth == 1 and lb_logits.shape[0] == 2
    batch_p, seq_p, d = x_prompt.shape
    batch_s, seq_s, _ = x_sample.shape
    assert seq_s == SUBLANES and LANES % seq_s == 0
    l = 0
    row = lambda a: a.reshape(1, -1)
    mixer_w = (row(norm_mix_w[l]), w_in[l].astype(BF16), row(hgrn_norm_w[l]), row(ln_v_w[l]),
               row(ln_v_b[l]), w_a_out[l].astype(BF16), w_b_out[l].astype(BF16),
               w_o[l].astype(BF16))
    ffn_w = (row(norm_ffn_w[l]), w_up[l].astype(BF16), conv_w[l], row(conv_b[l]),
             w_down[l].astype(BF16), row(norm_ple_w[l]), w_ple_gate[l].astype(BF16),
             w_ple_proj[l].astype(BF16), row(final_norm_w))

    ws, bs = w_spatial[l], b_spatial[l]
    wmix_p, bcol_p = ws, bs[:, :, None]
    reps = LANES // seq_s
    eye = jnp.eye(reps, dtype=F32)
    wmix_s = jnp.stack([jnp.kron(eye, ws[g, :seq_s, :seq_s]) for g in range(GROUPS)])
    bcol_s = jnp.tile(bs[:, :seq_s], (1, reps))[:, :, None]

    def run(x, p, st_h, st_c, wmix, bcol, rows, chunk):
        batch, seq, _ = x.shape
        outs = _mixer(x, st_h, lb_logits, mixer_w, wmix, bcol, rows=rows, chunk=chunk)
        y, conv_new = _ffn(outs[0], p.reshape(batch * seq, -1), st_c, ffn_w,
                           batch=batch, seq=seq, rows=rows)
        return (y.reshape(batch, seq, d), outs[1][None], conv_new[None]) + tuple(outs[2:])

    y_p, hs_p, cs_p = run(x_prompt, p_prompt[l], None, None, wmix_p, bcol_p, 256, 32)
    y_s, hs_s, cs_s, v_s = run(x_sample, p_sample[l], state_hgrn[l], state_conv[l],
                               wmix_s, bcol_s, LANES, seq_s)
    return (y_p, y_s, hs_p, hs_s, cs_p, cs_s, v_s.reshape(1, batch_s, seq_s, -1))
```

```python
import functools

import jax
import jax.numpy as jnp
from jax import lax
from jax.experimental import pallas as pl
from jax.experimental.pallas import tpu as pltpu

F32 = jnp.float32
BF16 = jnp.bfloat16
EPS = 1e-6

HEADS = 4
GROUPS = 4
CONV_W = 3
LANES = 128
SUBLANES = 8
FF_TILE = 256
VMEM_LIMIT = 56 * 1024 * 1024

NT_DIMS = (((1,), (1,)), ((), ()))
TN_DIMS = (((0,), (0,)), ((), ()))


def _rms(x, w):
    return x * lax.rsqrt(jnp.mean(x * x, axis=-1, keepdims=True) + EPS) * w


def _gelu(x):
    return 0.5 * x * (1.0 + lax.erf(x * (2.0 ** -0.5)))


def _dot(a, b):
    return jnp.dot(a, b, preferred_element_type=F32)


def _hgrn_head(q, k, b, v, st_in, chunk, scan):
    rows = q.shape[0]
    n = rows // chunk
    shift = chunk.bit_length() - 1
    b3 = b.reshape(n, chunk, LANES)

    def chunk_row(r):
        return jnp.broadcast_to(b3[:, r:r + 1, :], b3.shape).reshape(rows, LANES)

    def blocked(a):
        cols = []
        for j in range(n):
            parts = [jnp.zeros((j * chunk, LANES), F32), a[j * chunk:(j + 1) * chunk],
                     jnp.zeros((rows - (j + 1) * chunk, LANES), F32)]
            cols.append(jnp.concatenate([p for p in parts if p.shape[0]], axis=0))
        return jnp.concatenate(cols, axis=1).astype(BF16)

    b_mid = chunk_row(chunk // 2 - 1)
    b_end = chunk_row(chunk - 1)
    q_t = (q * jnp.exp(b - b_mid)).astype(BF16)
    k_t = (k * jnp.exp(b_mid - b)).astype(BF16)
    vb = v.astype(BF16)
    row = lax.broadcasted_iota(jnp.int32, (rows, rows), 0)
    col = lax.broadcasted_iota(jnp.int32, (rows, rows), 1)
    keep = ((row >> shift) == (col >> shift)) & (col <= row)
    scores = lax.dot_general(q_t, k_t, NT_DIMS, preferred_element_type=F32)
    scores = jnp.where(keep, scores, 0.0).astype(BF16)
    ds = lax.dot_general(vb, blocked(k * jnp.exp(b_end - b)), TN_DIMS, preferred_element_type=F32)
    before, after = [], []
    st = st_in
    for j in range(n):
        st_j = st if scan else st_in[j]
        before.append(st_j.astype(BF16))
        st = st_j * jnp.exp(b3[j, chunk - 1:chunk, :]) + ds[:, j * LANES:(j + 1) * LANES]
        after.append(st)
    o = _dot(scores, vb) + lax.dot_general(blocked(q * jnp.exp(b)), jnp.concatenate(before, axis=1),
                                           NT_DIMS, preferred_element_type=F32)
    return o, after


def _mixer_kernel(*refs, sample, rows, chunk, n_t):
    if sample:
        (x_ref, lbl_ref, nmw_ref, win_ref, hnw_ref, lvw_ref, lvb_ref, wmix_ref, bcol_ref,
         wa_ref, wb_ref, wo_ref, sin_ref, h_ref, sout_ref, vout_ref) = refs
    else:
        (x_ref, lbl_ref, nmw_ref, win_ref, hnw_ref, lvw_ref, lvb_ref, wmix_ref, bcol_ref,
         wa_ref, wb_ref, wo_ref, h_ref, sout_ref, st_s) = refs
    fdim = HEADS * LANES
    gw = GROUPS * LANES
    d_model = x_ref.shape[1]
    offs = [0]
    for width in (fdim, fdim, fdim, fdim, gw, gw, d_model, d_model):
        offs.append(offs[-1] + width)

    x = x_ref[...]
    xn = _rms(x, nmw_ref[...]).astype(BF16)

    def proj(i):
        return _dot(xn, win_ref[:, offs[i]:offs[i + 1]])

    lbl = lbl_ref[...]
    lbe = jnp.exp(lbl - jnp.max(lbl, axis=0, keepdims=True))
    lb = lbe[0:1, :] / jnp.sum(lbe, axis=0, keepdims=True)

    q = jax.nn.silu(proj(0))
    fg = lb + (1.0 - lb) * jax.nn.sigmoid(proj(1))
    k = 1.0 - fg
    v_in = proj(2)
    r128 = lax.broadcasted_iota(jnp.int32, (LANES, LANES), 0)
    c128 = lax.broadcasted_iota(jnp.int32, (LANES, LANES), 1)
    shift = min(chunk, LANES).bit_length() - 1
    seg = jnp.where(((r128 >> shift) == (c128 >> shift)) & (c128 <= r128), 1.0, 0.0).astype(F32)
    log_f = jnp.log(fg)
    b = jnp.concatenate(
        [jnp.dot(seg, log_f[g * LANES:(g + 1) * LANES, :], precision=lax.Precision.HIGHEST,
                 preferred_element_type=F32) for g in range(rows // LANES)], axis=0)

    if not sample:
        @pl.when(pl.program_id(1) == 0)
        def _():
            st_s[...] = jnp.zeros_like(st_s)

    n_chunks = rows // chunk
    o_heads = []
    for hd in range(HEADS):
        cols = slice(hd * LANES, (hd + 1) * LANES)
        st_in = [sin_ref[j, hd].T for j in range(n_chunks)] if sample else st_s[hd]
        o, st_out = _hgrn_head(q[:, cols], k[:, cols], b[:, cols], v_in[:, cols], st_in, chunk,
                               scan=not sample)
        o_heads.append(o)
        if sample:
            for j in range(n_chunks):
                sout_ref[j, hd] = st_out[j].T
        else:
            st_s[hd] = st_out[-1]
    o_rec = jnp.concatenate(o_heads, axis=1)

    if not sample:
        @pl.when(pl.program_id(1) == n_t - 1)
        def _():
            for hd in range(HEADS):
                sout_ref[0, hd] = st_s[hd].T

    o_a = _rms(o_rec * jax.nn.sigmoid(proj(3)), hnw_ref[...]).astype(BF16)

    u = _gelu(proj(4))
    gv = _gelu(proj(5))
    gc = gv - jnp.mean(gv, axis=-1, keepdims=True)
    v = gc * lax.rsqrt(jnp.mean(gc * gc, axis=-1, keepdims=True) + EPS) * lvw_ref[...] + lvb_ref[...]
    if sample:
        vout_ref[...] = v
    vb = v.astype(BF16)
    wm = [jnp.where(c128 <= r128, wmix_ref[g], 0.0).astype(BF16) for g in range(GROUPS)]
    bcol = [bcol_ref[g] for g in range(GROUPS)]
    mixed_rows = []
    for rg in range(rows // LANES):
        rsl = slice(rg * LANES, (rg + 1) * LANES)
        mixed_rows.append(jnp.concatenate(
            [_dot(wm[g], vb[rsl, g * LANES:(g + 1) * LANES]) + bcol[g] for g in range(GROUPS)],
            axis=1))
    o_b = (u * jnp.concatenate(mixed_rows, axis=0)).astype(BF16)

    mix = (jax.nn.sigmoid(proj(6)) * _dot(o_a, wa_ref[...])
           + jax.nn.sigmoid(proj(7)) * _dot(o_b, wb_ref[...])).astype(BF16)
    h_ref[...] = x + _dot(mix, wo_ref[...])


def _ffn_kernel(*refs, sample, rows, seq):
    if sample:
        (h_ref, p_ref, nfw_ref, wup_ref, cw_ref, cb_ref, wdn_ref, npw_ref, wpg_ref, wpp_ref,
         fnw_ref, cst_ref, y_ref, cout_ref) = refs
    else:
        (h_ref, p_ref, nfw_ref, wup_ref, cw_ref, cb_ref, wdn_ref, npw_ref, wpg_ref, wpp_ref,
         fnw_ref, y_ref, cout_ref, carry_s) = refs
        @pl.when(pl.program_id(1) == 0)
        def _():
            carry_s[...] = jnp.zeros_like(carry_s)
    d_ff = wdn_ref.shape[0]
    n_seq = rows // seq
    h = h_ref[...]
    hn = _rms(h, nfw_ref[...]).astype(BF16)
    rid = lax.broadcasted_iota(jnp.int32, (rows, FF_TILE), 0)
    tpos = (rid & (seq - 1)) if sample else rid

    def conv(up, lo):
        cols = slice(lo, lo + FF_TILE)
        w = cw_ref[:, cols]
        if sample:
            st = cst_ref[:, :, cols]
            e0 = jnp.broadcast_to(st[:, 0:1, :], (n_seq, seq, FF_TILE)).reshape(rows, FF_TILE)
            e1 = jnp.broadcast_to(st[:, 1:2, :], (n_seq, seq, FF_TILE)).reshape(rows, FF_TILE)
            cout_ref[:, :, cols] = up.reshape(n_seq, seq, FF_TILE)[:, seq - 2:seq, :]
        else:
            e0 = carry_s[0:1, cols]
            e1 = carry_s[1:2, cols]
            carry_s[:, cols] = up[rows - 2:rows, :]
            cout_ref[0, :, cols] = up[rows - 2:rows, :]
        x1 = jnp.where(tpos >= 1, pltpu.roll(up, 1, 0), e1)
        x2 = jnp.where(tpos >= 2, pltpu.roll(up, 2, 0), jnp.where(tpos == 1, e1, e0))
        return cb_ref[:, cols] + x2 * w[0:1, :] + x1 * w[1:2, :] + up * w[2:3, :]

    acc = jnp.zeros((rows, h.shape[1]), F32)
    for j in range(d_ff // FF_TILE):
        lo_a = j * FF_TILE
        lo_b = d_ff + j * FF_TILE
        ya = conv(_dot(hn, wup_ref[:, lo_a:lo_a + FF_TILE]), lo_a)
        yb = conv(_dot(hn, wup_ref[:, lo_b:lo_b + FF_TILE]), lo_b)
        acc = acc + _dot((_gelu(ya) * yb).astype(BF16), wdn_ref[lo_a:lo_a + FF_TILE, :])
    h2 = h + acc
    gate = jax.nn.sigmoid(_dot(_rms(h2, npw_ref[...]).astype(BF16), wpg_ref[...]))
    h3 = h2 + gate * _dot(p_ref[...].astype(BF16), wpp_ref[...])
    y_ref[...] = _rms(h3, fnw_ref[...])


def _const_spec(shape):
    nd = len(shape)
    return pl.BlockSpec(shape, lambda *_: (0,) * nd, pipeline_mode=pl.Buffered(1))


def _mixer(x, state, lb_logits, weights, wmix, bcol, *, rows, chunk):
    batch, seq, d = x.shape
    sample = state is not None
    x2 = x.reshape(batch * seq, d)
    fdim = HEADS * LANES
    const_args = (lb_logits,) + tuple(weights[:5]) + (wmix, bcol) + tuple(weights[5:])
    const_specs = [_const_spec(a.shape) for a in const_args]
    scratch = []
    if sample:
        n_seq = rows // seq
        grid = (batch // n_seq,)
        row_map = lambda i: (i, 0)
        st_spec = pl.BlockSpec((n_seq, HEADS, LANES, LANES), lambda i: (i, 0, 0, 0))
        in_specs = [pl.BlockSpec((rows, d), row_map)] + const_specs + [st_spec]
        out_specs = [pl.BlockSpec((rows, d), row_map), st_spec,
                     pl.BlockSpec((rows, GROUPS * LANES), row_map)]
        out_shape = [jax.ShapeDtypeStruct((batch * seq, d), F32),
                     jax.ShapeDtypeStruct(state.shape, F32),
                     jax.ShapeDtypeStruct((batch * seq, GROUPS * LANES), F32)]
        args = (x2,) + const_args + (state,)
        n_t = 1
        sem = ("arbitrary",)
    else:
        n_t = seq // rows
        grid = (batch, n_t)
        row_map = lambda b, t: (b * n_t + t, 0)
        in_specs = [pl.BlockSpec((rows, d), row_map)] + const_specs
        out_specs = [pl.BlockSpec((rows, d), row_map),
                     pl.BlockSpec((1, HEADS, LANES, LANES), lambda b, t: (b, 0, 0, 0))]
        out_shape = [jax.ShapeDtypeStruct((batch * seq, d), F32),
                     jax.ShapeDtypeStruct((batch, HEADS, LANES, LANES), F32)]
        scratch = [pltpu.VMEM((HEADS, LANES, LANES), F32)]
        args = (x2,) + const_args
        sem = ("arbitrary", "arbitrary")
    return pl.pallas_call(
        functools.partial(_mixer_kernel, sample=sample, rows=rows, chunk=chunk, n_t=n_t),
        grid=grid, in_specs=in_specs, out_specs=out_specs, out_shape=out_shape,
        scratch_shapes=scratch,
        compiler_params=pltpu.CompilerParams(dimension_semantics=sem,
                                             vmem_limit_bytes=VMEM_LIMIT),
        name="mixer_sample" if sample else "mixer_prompt",
    )(*args)


def _ffn(h2d, p2d, conv_state, weights, *, batch, seq, rows):
    d = h2d.shape[1]
    sample = conv_state is not None
    const_specs = [_const_spec(a.shape) for a in weights]
    width = weights[2].shape[1]
    if sample:
        n_seq = rows // seq
        grid = (batch // n_seq,)
        row_map = lambda i: (i, 0)
        cs_spec = pl.BlockSpec((n_seq, CONV_W - 1, width), lambda i: (i, 0, 0))
        in_specs = ([pl.BlockSpec((rows, d), row_map), pl.BlockSpec((rows, p2d.shape[1]), row_map)]
                    + const_specs + [cs_spec])
        out_specs = [pl.BlockSpec((rows, d), row_map), cs_spec]
        args = (h2d, p2d) + tuple(weights) + (conv_state,)
        scratch = []
        sem = ("arbitrary",)
    else:
        n_t = seq // rows
        grid = (batch, n_t)
        row_map = lambda b, t: (b * n_t + t, 0)
        in_specs = ([pl.BlockSpec((rows, d), row_map), pl.BlockSpec((rows, p2d.shape[1]), row_map)]
                    + const_specs)
        out_specs = [pl.BlockSpec((rows, d), row_map),
                     pl.BlockSpec((1, CONV_W - 1, width), lambda b, t: (b, 0, 0))]
        args = (h2d, p2d) + tuple(weights)
        scratch = [pltpu.VMEM((CONV_W - 1, width), F32)]
        sem = ("arbitrary", "arbitrary")
    out_shape = [jax.ShapeDtypeStruct((batch * seq, d), F32),
                 jax.ShapeDtypeStruct((batch, CONV_W - 1, width), F32)]
    return pl.pallas_call(
        functools.partial(_ffn_kernel, sample=sample, rows=rows, seq=seq),
        grid=grid, in_specs=in_specs, out_specs=out_specs, out_shape=out_shape,
        scratch_shapes=scratch,
        compiler_params=pltpu.CompilerParams(dimension_semantics=sem,
                                             vmem_limit_bytes=VMEM_LIMIT),
        name="ffn_sample" if sample else "ffn_prompt",
    )(*args)


def kernel(x_prompt, x_sample, p_prompt, p_sample, state_hgrn, state_conv, lb_logits, norm_mix_w, w_in, hgrn_norm_w, ln_v_w, ln_v_b, w_spatial, b_spatial, w_a_out, w_b_out, w_o, norm_ffn_w, w_up, conv_w, conv_b, w_down, norm_ple_w, w_ple_gate, w_ple_proj, final_norm_w):
    depth = w_in.shape[0]
    assert depth == 1 and lb_logits.shape[0] == 2
    batch_p, seq_p, d = x_prompt.shape
    batch_s, seq_s, _ = x_sample.shape
    assert seq_s == SUBLANES and LANES % seq_s == 0
    l = 0
    row = lambda a: a.reshape(1, -1)
    mixer_w = (row(norm_mix_w[l]), w_in[l].astype(BF16), row(hgrn_norm_w[l]), row(ln_v_w[l]),
               row(ln_v_b[l]), w_a_out[l].astype(BF16), w_b_out[l].astype(BF16),
               w_o[l].astype(BF16))
    ffn_w = (row(norm_ffn_w[l]), w_up[l].astype(BF16), conv_w[l], row(conv_b[l]),
             w_down[l].astype(BF16), row(norm_ple_w[l]), w_ple_gate[l].astype(BF16),
             w_ple_proj[l].astype(BF16), row(final_norm_w))

    ws, bs = w_spatial[l], b_spatial[l]
    wmix_p, bcol_p = ws, bs[:, :, None]
    reps = LANES // seq_s
    eye = jnp.eye(reps, dtype=F32)
    wmix_s = jnp.stack([jnp.kron(eye, ws[g, :seq_s, :seq_s]) for g in range(GROUPS)])
    bcol_s = jnp.tile(bs[:, :seq_s], (1, reps))[:, :, None]

    def run(x, p, st_h, st_c, wmix, bcol, rows, chunk):
        batch, seq, _ = x.shape
        outs = _mixer(x, st_h, lb_logits, mixer_w, wmix, bcol, rows=rows, chunk=chunk)
        y, conv_new = _ffn(outs[0], p.reshape(batch * seq, -1), st_c, ffn_w,
                           batch=batch, seq=seq, rows=rows)
        return (y.reshape(batch, seq, d), outs[1][None], conv_new[None]) + tuple(outs[2:])

    y_p, hs_p, cs_p = run(x_prompt, p_prompt[l], None, None, wmix_p, bcol_p, 256, 64)
    y_s, hs_s, cs_s, v_s = run(x_sample, p_sample[l], state_hgrn[l], state_conv[l],
                               wmix_s, bcol_s, LANES, seq_s)
    return (y_p, y_s, hs_p, hs_s, cs_p, cs_s, v_s.reshape(1, batch_s, seq_s, -1))
```

```python
import functools

import jax
import jax.numpy as jnp
from jax import lax
from jax.experimental import pallas as pl
from jax.experimental.pallas import tpu as pltpu

F32 = jnp.float32
BF16 = jnp.bfloat16
EPS = 1e-6

HEADS = 4
GROUPS = 4
CONV_W = 3
LANES = 128
SUBLANES = 8
FF_TILE = 256
VMEM_LIMIT = 56 * 1024 * 1024

NT_DIMS = (((1,), (1,)), ((), ()))
TN_DIMS = (((0,), (0,)), ((), ()))


def _rms(x, w):
    return x * lax.rsqrt(jnp.mean(x * x, axis=-1, keepdims=True) + EPS) * w


def _gelu(x):
    return 0.5 * x * (1.0 + lax.erf(x * (2.0 ** -0.5)))


def _dot(a, b):
    return jnp.dot(a, b, preferred_element_type=F32)


def _hgrn_head(q, k, b, v, st_in, chunk, scan):
    rows = q.shape[0]
    n = rows // chunk
    shift = chunk.bit_length() - 1
    b3 = b.reshape(n, chunk, LANES)

    def chunk_row(r):
        return jnp.broadcast_to(b3[:, r:r + 1, :], b3.shape).reshape(rows, LANES)

    def blocked(a):
        cols = []
        for j in range(n):
            parts = [jnp.zeros((j * chunk, LANES), F32), a[j * chunk:(j + 1) * chunk],
                     jnp.zeros((rows - (j + 1) * chunk, LANES), F32)]
            cols.append(jnp.concatenate([p for p in parts if p.shape[0]], axis=0))
        return jnp.concatenate(cols, axis=1).astype(BF16)

    b_mid = chunk_row(chunk // 2 - 1)
    b_end = chunk_row(chunk - 1)
    q_t = (q * jnp.exp(b - b_mid)).astype(BF16)
    k_t = (k * jnp.exp(b_mid - b)).astype(BF16)
    vb = v.astype(BF16)
    row = lax.broadcasted_iota(jnp.int32, (rows, rows), 0)
    col = lax.broadcasted_iota(jnp.int32, (rows, rows), 1)
    keep = ((row >> shift) == (col >> shift)) & (col <= row)
    scores = lax.dot_general(q_t, k_t, NT_DIMS, preferred_element_type=F32)
    scores = jnp.where(keep, scores, 0.0).astype(BF16)
    ds = lax.dot_general(vb, blocked(k * jnp.exp(b_end - b)), TN_DIMS, preferred_element_type=F32)
    before, after = [], []
    st = st_in
    for j in range(n):
        st_j = st if scan else st_in[j]
        before.append(st_j.astype(BF16))
        st = st_j * jnp.exp(b3[j, chunk - 1:chunk, :]) + ds[:, j * LANES:(j + 1) * LANES]
        after.append(st)
    o = _dot(scores, vb) + lax.dot_general(blocked(q * jnp.exp(b)), jnp.concatenate(before, axis=1),
                                           NT_DIMS, preferred_element_type=F32)
    return o, after


def _mixer_kernel(*refs, sample, rows, chunk, n_t):
    if sample:
        (x_ref, lbl_ref, nmw_ref, win_ref, hnw_ref, lvw_ref, lvb_ref, wmix_ref, bcol_ref,
         wa_ref, wb_ref, wo_ref, sin_ref, h_ref, sout_ref, vout_ref) = refs
    else:
        (x_ref, lbl_ref, nmw_ref, win_ref, hnw_ref, lvw_ref, lvb_ref, wmix_ref, bcol_ref,
         wa_ref, wb_ref, wo_ref, h_ref, sout_ref, st_s) = refs
    fdim = HEADS * LANES
    gw = GROUPS * LANES
    d_model = x_ref.shape[1]
    offs = [0]
    for width in (fdim, fdim, fdim, fdim, gw, gw, d_model, d_model):
        offs.append(offs[-1] + width)

    x = x_ref[...]
    xn = _rms(x, nmw_ref[...]).astype(BF16)

    def proj(i):
        return _dot(xn, win_ref[:, offs[i]:offs[i + 1]])

    lbl = lbl_ref[...]
    lbe = jnp.exp(lbl - jnp.max(lbl, axis=0, keepdims=True))
    lb = lbe[0:1, :] / jnp.sum(lbe, axis=0, keepdims=True)

    q = jax.nn.silu(proj(0))
    fg = lb + (1.0 - lb) * jax.nn.sigmoid(proj(1))
    k = 1.0 - fg
    v_in = proj(2)
    r128 = lax.broadcasted_iota(jnp.int32, (LANES, LANES), 0)
    c128 = lax.broadcasted_iota(jnp.int32, (LANES, LANES), 1)
    shift = min(chunk, LANES).bit_length() - 1
    seg = jnp.where(((r128 >> shift) == (c128 >> shift)) & (c128 <= r128), 1.0, 0.0).astype(F32)
    log_f = jnp.log(fg)
    b = jnp.concatenate(
        [jnp.dot(seg, log_f[g * LANES:(g + 1) * LANES, :], precision=lax.Precision.HIGHEST,
                 preferred_element_type=F32) for g in range(rows // LANES)], axis=0)

    if not sample:
        @pl.when(pl.program_id(1) == 0)
        def _():
            st_s[...] = jnp.zeros_like(st_s)

    n_chunks = rows // chunk

    def head(hd):
        cols = slice(hd * LANES, (hd + 1) * LANES)
        st_in = [sin_ref[j, hd].T for j in range(n_chunks)] if sample else st_s[hd]
        o, st_out = _hgrn_head(q[:, cols], k[:, cols], b[:, cols], v_in[:, cols], st_in, chunk,
                               scan=not sample)
        if sample:
            for j in range(n_chunks):
                sout_ref[j, hd] = st_out[j].T
        else:
            st_s[hd] = st_out[-1]
        return o

    o_heads = [head(0)]
    og = jax.nn.sigmoid(proj(3))
    o_heads.append(head(1))
    u = _gelu(proj(4))
    o_heads.append(head(2))
    gv = _gelu(proj(5))
    gc = gv - jnp.mean(gv, axis=-1, keepdims=True)
    v = gc * lax.rsqrt(jnp.mean(gc * gc, axis=-1, keepdims=True) + EPS) * lvw_ref[...] + lvb_ref[...]
    o_heads.append(head(3))
    gate_a = jax.nn.sigmoid(proj(6))
    gate_b = jax.nn.sigmoid(proj(7))

    if not sample:
        @pl.when(pl.program_id(1) == n_t - 1)
        def _():
            for hd in range(HEADS):
                sout_ref[0, hd] = st_s[hd].T

    o_a = _rms(jnp.concatenate(o_heads, axis=1) * og, hnw_ref[...]).astype(BF16)

    if sample:
        vout_ref[...] = v
    vb = v.astype(BF16)
    wm = [jnp.where(c128 <= r128, wmix_ref[g], 0.0).astype(BF16) for g in range(GROUPS)]
    bcol = [bcol_ref[g] for g in range(GROUPS)]
    mixed_rows = []
    for rg in range(rows // LANES):
        rsl = slice(rg * LANES, (rg + 1) * LANES)
        mixed_rows.append(jnp.concatenate(
            [_dot(wm[g], vb[rsl, g * LANES:(g + 1) * LANES]) + bcol[g] for g in range(GROUPS)],
            axis=1))
    o_b = (u * jnp.concatenate(mixed_rows, axis=0)).astype(BF16)

    mix = (gate_a * _dot(o_a, wa_ref[...]) + gate_b * _dot(o_b, wb_ref[...])).astype(BF16)
    h_ref[...] = x + _dot(mix, wo_ref[...])


def _ffn_kernel(*refs, sample, rows, seq):
    if sample:
        (h_ref, p_ref, nfw_ref, wup_ref, cw_ref, cb_ref, wdn_ref, npw_ref, wpg_ref, wpp_ref,
         fnw_ref, cst_ref, y_ref, cout_ref) = refs
    else:
        (h_ref, p_ref, nfw_ref, wup_ref, cw_ref, cb_ref, wdn_ref, npw_ref, wpg_ref, wpp_ref,
         fnw_ref, y_ref, cout_ref, carry_s) = refs
        @pl.when(pl.program_id(1) == 0)
        def _():
            carry_s[...] = jnp.zeros_like(carry_s)
    d_ff = wdn_ref.shape[0]
    n_seq = rows // seq
    h = h_ref[...]
    hn = _rms(h, nfw_ref[...]).astype(BF16)
    rid = lax.broadcasted_iota(jnp.int32, (rows, FF_TILE), 0)
    tpos = (rid & (seq - 1)) if sample else rid

    def conv(up, lo):
        cols = slice(lo, lo + FF_TILE)
        w = cw_ref[:, cols]
        if sample:
            st = cst_ref[:, :, cols]
            e0 = jnp.broadcast_to(st[:, 0:1, :], (n_seq, seq, FF_TILE)).reshape(rows, FF_TILE)
            e1 = jnp.broadcast_to(st[:, 1:2, :], (n_seq, seq, FF_TILE)).reshape(rows, FF_TILE)
            cout_ref[:, :, cols] = up.reshape(n_seq, seq, FF_TILE)[:, seq - 2:seq, :]
        else:
            e0 = carry_s[0:1, cols]
            e1 = carry_s[1:2, cols]
            carry_s[:, cols] = up[rows - 2:rows, :]
            cout_ref[0, :, cols] = up[rows - 2:rows, :]
        x1 = jnp.where(tpos >= 1, pltpu.roll(up, 1, 0), e1)
        x2 = jnp.where(tpos >= 2, pltpu.roll(up, 2, 0), jnp.where(tpos == 1, e1, e0))
        return cb_ref[:, cols] + x2 * w[0:1, :] + x1 * w[1:2, :] + up * w[2:3, :]

    n_tiles = d_ff // FF_TILE

    def up_pair(j):
        lo_a = j * FF_TILE
        lo_b = d_ff + j * FF_TILE
        return (_dot(hn, wup_ref[:, lo_a:lo_a + FF_TILE]), _dot(hn, wup_ref[:, lo_b:lo_b + FF_TILE]))

    acc = jnp.zeros((rows, h.shape[1]), F32)
    nxt = up_pair(0)
    gated = None
    for j in range(n_tiles + 1):
        lo_a = j * FF_TILE
        cur = nxt
        if j + 1 < n_tiles:
            nxt = up_pair(j + 1)
        if j < n_tiles:
            ya = conv(cur[0], lo_a)
            yb = conv(cur[1], d_ff + lo_a)
            new_gated = (_gelu(ya) * yb).astype(BF16)
        if j > 0:
            acc = acc + _dot(gated, wdn_ref[lo_a - FF_TILE:lo_a, :])
        gated = new_gated
    h2 = h + acc
    gate = jax.nn.sigmoid(_dot(_rms(h2, npw_ref[...]).astype(BF16), wpg_ref[...]))
    h3 = h2 + gate * _dot(p_ref[...].astype(BF16), wpp_ref[...])
    y_ref[...] = _rms(h3, fnw_ref[...])


def _const_spec(shape):
    nd = len(shape)
    return pl.BlockSpec(shape, lambda *_: (0,) * nd, pipeline_mode=pl.Buffered(1))


def _mixer(x, state, lb_logits, weights, wmix, bcol, *, rows, chunk):
    batch, seq, d = x.shape
    sample = state is not None
    x2 = x.reshape(batch * seq, d)
    fdim = HEADS * LANES
    const_args = (lb_logits,) + tuple(weights[:5]) + (wmix, bcol) + tuple(weights[5:])
    const_specs = [_const_spec(a.shape) for a in const_args]
    scratch = []
    if sample:
        n_seq = rows // seq
        grid = (batch // n_seq,)
        row_map = lambda i: (i, 0)
        st_spec = pl.BlockSpec((n_seq, HEADS, LANES, LANES), lambda i: (i, 0, 0, 0))
        in_specs = [pl.BlockSpec((rows, d), row_map)] + const_specs + [st_spec]
        out_specs = [pl.BlockSpec((rows, d), row_map), st_spec,
                     pl.BlockSpec((rows, GROUPS * LANES), row_map)]
        out_shape = [jax.ShapeDtypeStruct((batch * seq, d), F32),
                     jax.ShapeDtypeStruct(state.shape, F32),
                     jax.ShapeDtypeStruct((batch * seq, GROUPS * LANES), F32)]
        args = (x2,) + const_args + (state,)
        n_t = 1
        sem = ("arbitrary",)
    else:
        n_t = seq // rows
        grid = (batch, n_t)
        row_map = lambda b, t: (b * n_t + t, 0)
        in_specs = [pl.BlockSpec((rows, d), row_map)] + const_specs
        out_specs = [pl.BlockSpec((rows, d), row_map),
                     pl.BlockSpec((1, HEADS, LANES, LANES), lambda b, t: (b, 0, 0, 0))]
        out_shape = [jax.ShapeDtypeStruct((batch * seq, d), F32),
                     jax.ShapeDtypeStruct((batch, HEADS, LANES, LANES), F32)]
        scratch = [pltpu.VMEM((HEADS, LANES, LANES), F32)]
        args = (x2,) + const_args
        sem = ("arbitrary", "arbitrary")
    return pl.pallas_call(
        functools.partial(_mixer_kernel, sample=sample, rows=rows, chunk=chunk, n_t=n_t),
        grid=grid, in_specs=in_specs, out_specs=out_specs, out_shape=out_shape,
        scratch_shapes=scratch,
        compiler_params=pltpu.CompilerParams(dimension_semantics=sem,
                                             vmem_limit_bytes=VMEM_LIMIT),
        name="mixer_sample" if sample else "mixer_prompt",
    )(*args)


def _ffn(h2d, p2d, conv_state, weights, *, batch, seq, rows):
    d = h2d.shape[1]
    sample = conv_state is not None
    const_specs = [_const_spec(a.shape) for a in weights]
    width = weights[2].shape[1]
    if sample:
        n_seq = rows // seq
        grid = (batch // n_seq,)
        row_map = lambda i: (i, 0)
        cs_spec = pl.BlockSpec((n_seq, CONV_W - 1, width), lambda i: (i, 0, 0))
        in_specs = ([pl.BlockSpec((rows, d), row_map), pl.BlockSpec((rows, p2d.shape[1]), row_map)]
                    + const_specs + [cs_spec])
        out_specs = [pl.BlockSpec((rows, d), row_map), cs_spec]
        args = (h2d, p2d) + tuple(weights) + (conv_state,)
        scratch = []
        sem = ("arbitrary",)
    else:
        n_t = seq // rows
        grid = (batch, n_t)
        row_map = lambda b, t: (b * n_t + t, 0)
        in_specs = ([pl.BlockSpec((rows, d), row_map), pl.BlockSpec((rows, p2d.shape[1]), row_map)]
                    + const_specs)
        out_specs = [pl.BlockSpec((rows, d), row_map),
                     pl.BlockSpec((1, CONV_W - 1, width), lambda b, t: (b, 0, 0))]
        args = (h2d, p2d) + tuple(weights)
        scratch = [pltpu.VMEM((CONV_W - 1, width), F32)]
        sem = ("arbitrary", "arbitrary")
    out_shape = [jax.ShapeDtypeStruct((batch * seq, d), F32),
                 jax.ShapeDtypeStruct((batch, CONV_W - 1, width), F32)]
    return pl.pallas_call(
        functools.partial(_ffn_kernel, sample=sample, rows=rows, seq=seq),
        grid=grid, in_specs=in_specs, out_specs=out_specs, out_shape=out_shape,
        scratch_shapes=scratch,
        compiler_params=pltpu.CompilerParams(dimension_semantics=sem,
                                             vmem_limit_bytes=VMEM_LIMIT),
        name="ffn_sample" if sample else "ffn_prompt",
    )(*args)


def kernel(x_prompt, x_sample, p_prompt, p_sample, state_hgrn, state_conv, lb_logits, norm_mix_w, w_in, hgrn_norm_w, ln_v_w, ln_v_b, w_spatial, b_spatial, w_a_out, w_b_out, w_o, norm_ffn_w, w_up, conv_w, conv_b, w_down, norm_ple_w, w_ple_gate, w_ple_proj, final_norm_w):
    depth = w_in.shape[0]
    assert depth == 1 and lb_logits.shape[0] == 2
    batch_p, seq_p, d = x_prompt.shape
    batch_s, seq_s, _ = x_sample.shape
    assert seq_s == SUBLANES and LANES % seq_s == 0
    l = 0
    row = lambda a: a.reshape(1, -1)
    mixer_w = (row(norm_mix_w[l]), w_in[l].astype(BF16), row(hgrn_norm_w[l]), row(ln_v_w[l]),
               row(ln_v_b[l]), w_a_out[l].astype(BF16), w_b_out[l].astype(BF16),
               w_o[l].astype(BF16))
    ffn_w = (row(norm_ffn_w[l]), w_up[l].astype(BF16), conv_w[l], row(conv_b[l]),
             w_down[l].astype(BF16), row(norm_ple_w[l]), w_ple_gate[l].astype(BF16),
             w_ple_proj[l].astype(BF16), row(final_norm_w))

    ws, bs = w_spatial[l], b_spatial[l]
    wmix_p, bcol_p = ws, bs[:, :, None]
    reps = LANES // seq_s
    eye = jnp.eye(reps, dtype=F32)
    wmix_s = jnp.stack([jnp.kron(eye, ws[g, :seq_s, :seq_s]) for g in range(GROUPS)])
    bcol_s = jnp.tile(bs[:, :seq_s], (1, reps))[:, :, None]

    def run(x, p, st_h, st_c, wmix, bcol, rows, chunk):
        batch, seq, _ = x.shape
        outs = _mixer(x, st_h, lb_logits, mixer_w, wmix, bcol, rows=rows, chunk=chunk)
        y, conv_new = _ffn(outs[0], p.reshape(batch * seq, -1), st_c, ffn_w,
                           batch=batch, seq=seq, rows=rows)
        return (y.reshape(batch, seq, d), outs[1][None], conv_new[None]) + tuple(outs[2:])

    y_p, hs_p, cs_p = run(x_prompt, p_prompt[l], None, None, wmix_p, bcol_p, 256, 64)
    y_s, hs_s, cs_s, v_s = run(x_sample, p_sample[l], state_hgrn[l], state_conv[l],
                               wmix_s, bcol_s, LANES, seq_s)
    return (y_p, y_s, hs_p, hs_s, cs_p, cs_s, v_s.reshape(1, batch_s, seq_s, -1))
```

```python
import functools

import jax
import jax.numpy as jnp
from jax import lax
from jax.experimental import pallas as pl
from jax.experimental.pallas import tpu as pltpu

F32 = jnp.float32
BF16 = jnp.bfloat16
EPS = 1e-6

HEADS = 4
GROUPS = 4
CONV_W = 3
LANES = 128
SUBLANES = 8
FF_TILE = 256
VMEM_LIMIT = 56 * 1024 * 1024

NT_DIMS = (((1,), (1,)), ((), ()))
TN_DIMS = (((0,), (0,)), ((), ()))


def _rms(x, w):
    return x * lax.rsqrt(jnp.mean(x * x, axis=-1, keepdims=True) + EPS) * w


def _gelu(x):
    return 0.5 * x * (1.0 + lax.erf(x * (2.0 ** -0.5)))


def _dot(a, b):
    return jnp.dot(a, b, preferred_element_type=F32)


def _hgrn_head(q, k, b, v, st_in, chunk, scan):
    rows = q.shape[0]
    n = rows // chunk
    shift = chunk.bit_length() - 1
    b3 = b.reshape(n, chunk, LANES)

    def chunk_row(r):
        return jnp.broadcast_to(b3[:, r:r + 1, :], b3.shape).reshape(rows, LANES)

    def blocked(a):
        cols = []
        for j in range(n):
            parts = [jnp.zeros((j * chunk, LANES), F32), a[j * chunk:(j + 1) * chunk],
                     jnp.zeros((rows - (j + 1) * chunk, LANES), F32)]
            cols.append(jnp.concatenate([p for p in parts if p.shape[0]], axis=0))
        return jnp.concatenate(cols, axis=1).astype(BF16)

    b_mid = chunk_row(chunk // 2 - 1)
    b_end = chunk_row(chunk - 1)
    q_t = (q * jnp.exp(b - b_mid)).astype(BF16)
    k_t = (k * jnp.exp(b_mid - b)).astype(BF16)
    vb = v.astype(BF16)
    row = lax.broadcasted_iota(jnp.int32, (rows, rows), 0)
    col = lax.broadcasted_iota(jnp.int32, (rows, rows), 1)
    keep = ((row >> shift) == (col >> shift)) & (col <= row)
    scores = lax.dot_general(q_t, k_t, NT_DIMS, preferred_element_type=F32)
    scores = jnp.where(keep, scores, 0.0).astype(BF16)
    ds = lax.dot_general(vb, blocked(k * jnp.exp(b_end - b)), TN_DIMS, preferred_element_type=F32)
    before, after = [], []
    st = st_in
    for j in range(n):
        st_j = st if scan else st_in[j]
        before.append(st_j.astype(BF16))
        st = st_j * jnp.exp(b3[j, chunk - 1:chunk, :]) + ds[:, j * LANES:(j + 1) * LANES]
        after.append(st)
    o = _dot(scores, vb) + lax.dot_general(blocked(q * jnp.exp(b)), jnp.concatenate(before, axis=1),
                                           NT_DIMS, preferred_element_type=F32)
    return o, after


def _mixer_kernel(*refs, sample, rows, chunk, n_t):
    if sample:
        (x_ref, lbl_ref, nmw_ref, win_ref, hnw_ref, lvw_ref, lvb_ref, wmix_ref, bcol_ref,
         wa_ref, wb_ref, wo_ref, sin_ref, h_ref, sout_ref, vout_ref) = refs
    else:
        (x_ref, lbl_ref, nmw_ref, win_ref, hnw_ref, lvw_ref, lvb_ref, wmix_ref, bcol_ref,
         wa_ref, wb_ref, wo_ref, h_ref, sout_ref, st_s) = refs
    fdim = HEADS * LANES
    gw = GROUPS * LANES
    d_model = x_ref.shape[1]
    offs = [0]
    for width in (fdim, fdim, fdim, fdim, gw, gw, d_model, d_model):
        offs.append(offs[-1] + width)

    x = x_ref[...]
    xn = _rms(x, nmw_ref[...]).astype(BF16)

    def proj(i):
        return _dot(xn, win_ref[:, offs[i]:offs[i + 1]])

    lbl = lbl_ref[...]
    lbe = jnp.exp(lbl - jnp.max(lbl, axis=0, keepdims=True))
    lb = lbe[0:1, :] / jnp.sum(lbe, axis=0, keepdims=True)

    q = jax.nn.silu(proj(0))
    fg = lb + (1.0 - lb) * jax.nn.sigmoid(proj(1))
    k = 1.0 - fg
    v_in = proj(2)
    r128 = lax.broadcasted_iota(jnp.int32, (LANES, LANES), 0)
    c128 = lax.broadcasted_iota(jnp.int32, (LANES, LANES), 1)
    shift = min(chunk, LANES).bit_length() - 1
    seg = jnp.where(((r128 >> shift) == (c128 >> shift)) & (c128 <= r128), 1.0, 0.0).astype(BF16)
    log_f = jnp.log(fg)
    lf_hi = log_f.astype(BF16)
    lf_rem = log_f - lf_hi.astype(F32)
    lf_mid = lf_rem.astype(BF16)
    lf_lo = (lf_rem - lf_mid.astype(F32)).astype(BF16)
    lf3 = jnp.concatenate([lf_hi, lf_mid, lf_lo], axis=1)
    b_rows = []
    for g in range(rows // LANES):
        b3 = _dot(seg, lf3[g * LANES:(g + 1) * LANES, :])
        b_rows.append(b3[:, :fdim] + b3[:, fdim:2 * fdim] + b3[:, 2 * fdim:])
    b = jnp.concatenate(b_rows, axis=0)

    if not sample:
        @pl.when(pl.program_id(1) == 0)
        def _():
            st_s[...] = jnp.zeros_like(st_s)

    n_chunks = rows // chunk

    def head(hd):
        cols = slice(hd * LANES, (hd + 1) * LANES)
        st_in = [sin_ref[j, hd].T for j in range(n_chunks)] if sample else st_s[hd]
        o, st_out = _hgrn_head(q[:, cols], k[:, cols], b[:, cols], v_in[:, cols], st_in, chunk,
                               scan=not sample)
        if sample:
            for j in range(n_chunks):
                sout_ref[j, hd] = st_out[j].T
        else:
            st_s[hd] = st_out[-1]
        return o

    o_heads = [head(0)]
    og = jax.nn.sigmoid(proj(3))
    o_heads.append(head(1))
    u = _gelu(proj(4))
    o_heads.append(head(2))
    gv = _gelu(proj(5))
    gc = gv - jnp.mean(gv, axis=-1, keepdims=True)
    v = gc * lax.rsqrt(jnp.mean(gc * gc, axis=-1, keepdims=True) + EPS) * lvw_ref[...] + lvb_ref[...]
    o_heads.append(head(3))
    gate_a = jax.nn.sigmoid(proj(6))
    gate_b = jax.nn.sigmoid(proj(7))

    if not sample:
        @pl.when(pl.program_id(1) == n_t - 1)
        def _():
            for hd in range(HEADS):
                sout_ref[0, hd] = st_s[hd].T

    o_a = _rms(jnp.concatenate(o_heads, axis=1) * og, hnw_ref[...]).astype(BF16)

    if sample:
        vout_ref[...] = v
    vb = v.astype(BF16)
    wm = [jnp.where(c128 <= r128, wmix_ref[g], 0.0).astype(BF16) for g in range(GROUPS)]
    bcol = [bcol_ref[g] for g in range(GROUPS)]
    n_rg = rows // LANES
    mixed_g = [
        _dot(wm[g], jnp.concatenate(
            [vb[rg * LANES:(rg + 1) * LANES, g * LANES:(g + 1) * LANES] for rg in range(n_rg)],
            axis=1)) for g in range(GROUPS)]
    mixed = jnp.concatenate(
        [jnp.concatenate([mixed_g[g][:, rg * LANES:(rg + 1) * LANES] + bcol[g]
                          for g in range(GROUPS)], axis=1) for rg in range(n_rg)], axis=0)
    o_b = (u * mixed).astype(BF16)

    mix = (gate_a * _dot(o_a, wa_ref[...]) + gate_b * _dot(o_b, wb_ref[...])).astype(BF16)
    h_ref[...] = x + _dot(mix, wo_ref[...])


def _ffn_kernel(*refs, sample, rows, seq):
    if sample:
        (h_ref, p_ref, nfw_ref, wup_ref, cw_ref, cb_ref, wdn_ref, npw_ref, wpg_ref, wpp_ref,
         fnw_ref, cst_ref, y_ref, cout_ref) = refs
    else:
        (h_ref, p_ref, nfw_ref, wup_ref, cw_ref, cb_ref, wdn_ref, npw_ref, wpg_ref, wpp_ref,
         fnw_ref, y_ref, cout_ref, carry_s) = refs
        @pl.when(pl.program_id(1) == 0)
        def _():
            carry_s[...] = jnp.zeros_like(carry_s)
    d_ff = wdn_ref.shape[0]
    n_seq = rows // seq
    h = h_ref[...]
    hn = _rms(h, nfw_ref[...]).astype(BF16)
    rid = lax.broadcasted_iota(jnp.int32, (rows, FF_TILE), 0)
    tpos = (rid & (seq - 1)) if sample else rid

    def conv(up, lo):
        cols = slice(lo, lo + FF_TILE)
        w = cw_ref[:, cols]
        if sample:
            st = cst_ref[:, :, cols]
            e0 = jnp.broadcast_to(st[:, 0:1, :], (n_seq, seq, FF_TILE)).reshape(rows, FF_TILE)
            e1 = jnp.broadcast_to(st[:, 1:2, :], (n_seq, seq, FF_TILE)).reshape(rows, FF_TILE)
            cout_ref[:, :, cols] = up.reshape(n_seq, seq, FF_TILE)[:, seq - 2:seq, :]
        else:
            e0 = carry_s[0:1, cols]
            e1 = carry_s[1:2, cols]
            carry_s[:, cols] = up[rows - 2:rows, :]
            cout_ref[0, :, cols] = up[rows - 2:rows, :]
        x1 = jnp.where(tpos >= 1, pltpu.roll(up, 1, 0), e1)
        x2 = jnp.where(tpos >= 2, pltpu.roll(up, 2, 0), jnp.where(tpos == 1, e1, e0))
        return cb_ref[:, cols] + x2 * w[0:1, :] + x1 * w[1:2, :] + up * w[2:3, :]

    n_tiles = d_ff // FF_TILE

    def up_pair(j):
        lo_a = j * FF_TILE
        lo_b = d_ff + j * FF_TILE
        return (_dot(hn, wup_ref[:, lo_a:lo_a + FF_TILE]), _dot(hn, wup_ref[:, lo_b:lo_b + FF_TILE]))

    acc = jnp.zeros((rows, h.shape[1]), F32)
    nxt = up_pair(0)
    gated = None
    for j in range(n_tiles + 1):
        lo_a = j * FF_TILE
        cur = nxt
        if j + 1 < n_tiles:
            nxt = up_pair(j + 1)
        if j < n_tiles:
            ya = conv(cur[0], lo_a)
            yb = conv(cur[1], d_ff + lo_a)
            new_gated = (_gelu(ya) * yb).astype(BF16)
        if j > 0:
            acc = acc + _dot(gated, wdn_ref[lo_a - FF_TILE:lo_a, :])
        gated = new_gated
    h2 = h + acc
    gate = jax.nn.sigmoid(_dot(_rms(h2, npw_ref[...]).astype(BF16), wpg_ref[...]))
    h3 = h2 + gate * _dot(p_ref[...].astype(BF16), wpp_ref[...])
    y_ref[...] = _rms(h3, fnw_ref[...])


def _const_spec(shape):
    nd = len(shape)
    return pl.BlockSpec(shape, lambda *_: (0,) * nd, pipeline_mode=pl.Buffered(1))


def _mixer(x, state, lb_logits, weights, wmix, bcol, *, rows, chunk):
    batch, seq, d = x.shape
    sample = state is not None
    x2 = x.reshape(batch * seq, d)
    fdim = HEADS * LANES
    const_args = (lb_logits,) + tuple(weights[:5]) + (wmix, bcol) + tuple(weights[5:])
    const_specs = [_const_spec(a.shape) for a in const_args]
    scratch = []
    if sample:
        n_seq = rows // seq
        grid = (batch // n_seq,)
        row_map = lambda i: (i, 0)
        st_spec = pl.BlockSpec((n_seq, HEADS, LANES, LANES), lambda i: (i, 0, 0, 0))
        in_specs = [pl.BlockSpec((rows, d), row_map)] + const_specs + [st_spec]
        out_specs = [pl.BlockSpec((rows, d), row_map), st_spec,
                     pl.BlockSpec((rows, GROUPS * LANES), row_map)]
        out_shape = [jax.ShapeDtypeStruct((batch * seq, d), F32),
                     jax.ShapeDtypeStruct(state.shape, F32),
                     jax.ShapeDtypeStruct((batch * seq, GROUPS * LANES), F32)]
        args = (x2,) + const_args + (state,)
        n_t = 1
        sem = ("arbitrary",)
    else:
        n_t = seq // rows
        grid = (batch, n_t)
        row_map = lambda b, t: (b * n_t + t, 0)
        in_specs = [pl.BlockSpec((rows, d), row_map)] + const_specs
        out_specs = [pl.BlockSpec((rows, d), row_map),
                     pl.BlockSpec((1, HEADS, LANES, LANES), lambda b, t: (b, 0, 0, 0))]
        out_shape = [jax.ShapeDtypeStruct((batch * seq, d), F32),
                     jax.ShapeDtypeStruct((batch, HEADS, LANES, LANES), F32)]
        scratch = [pltpu.VMEM((HEADS, LANES, LANES), F32)]
        args = (x2,) + const_args
        sem = ("arbitrary", "arbitrary")
    return pl.pallas_call(
        functools.partial(_mixer_kernel, sample=sample, rows=rows, chunk=chunk, n_t=n_t),
        grid=grid, in_specs=in_specs, out_specs=out_specs, out_shape=out_shape,
        scratch_shapes=scratch,
        compiler_params=pltpu.CompilerParams(dimension_semantics=sem,
                                             vmem_limit_bytes=VMEM_LIMIT),
        name="mixer_sample" if sample else "mixer_prompt",
    )(*args)


def _ffn(h2d, p2d, conv_state, weights, *, batch, seq, rows):
    d = h2d.shape[1]
    sample = conv_state is not None
    const_specs = [_const_spec(a.shape) for a in weights]
    width = weights[2].shape[1]
    if sample:
        n_seq = rows // seq
        grid = (batch // n_seq,)
        row_map = lambda i: (i, 0)
        cs_spec = pl.BlockSpec((n_seq, CONV_W - 1, width), lambda i: (i, 0, 0))
        in_specs = ([pl.BlockSpec((rows, d), row_map), pl.BlockSpec((rows, p2d.shape[1]), row_map)]
                    + const_specs + [cs_spec])
        out_specs = [pl.BlockSpec((rows, d), row_map), cs_spec]
        args = (h2d, p2d) + tuple(weights) + (conv_state,)
        scratch = []
        sem = ("arbitrary",)
    else:
        n_t = seq // rows
        grid = (batch, n_t)
        row_map = lambda b, t: (b * n_t + t, 0)
        in_specs = ([pl.BlockSpec((rows, d), row_map), pl.BlockSpec((rows, p2d.shape[1]), row_map)]
                    + const_specs)
        out_specs = [pl.BlockSpec((rows, d), row_map),
                     pl.BlockSpec((1, CONV_W - 1, width), lambda b, t: (b, 0, 0))]
        args = (h2d, p2d) + tuple(weights)
        scratch = [pltpu.VMEM((CONV_W - 1, width), F32)]
        sem = ("arbitrary", "arbitrary")
    out_shape = [jax.ShapeDtypeStruct((batch * seq, d), F32),
                 jax.ShapeDtypeStruct((batch, CONV_W - 1, width), F32)]
    return pl.pallas_call(
        functools.partial(_ffn_kernel, sample=sample, rows=rows, seq=seq),
        grid=grid, in_specs=in_specs, out_specs=out_specs, out_shape=out_shape,
        scratch_shapes=scratch,
        compiler_params=pltpu.CompilerParams(dimension_semantics=sem,
                                             vmem_limit_bytes=VMEM_LIMIT),
        name="ffn_sample" if sample else "ffn_prompt",
    )(*args)


def kernel(x_prompt, x_sample, p_prompt, p_sample, state_hgrn, state_conv, lb_logits, norm_mix_w, w_in, hgrn_norm_w, ln_v_w, ln_v_b, w_spatial, b_spatial, w_a_out, w_b_out, w_o, norm_ffn_w, w_up, conv_w, conv_b, w_down, norm_ple_w, w_ple_gate, w_ple_proj, final_norm_w):
    depth = w_in.shape[0]
    assert depth == 1 and lb_logits.shape[0] == 2
    batch_p, seq_p, d = x_prompt.shape
    batch_s, seq_s, _ = x_sample.shape
    assert seq_s == SUBLANES and LANES % seq_s == 0
    l = 0
    row = lambda a: a.reshape(1, -1)
    mixer_w = (row(norm_mix_w[l]), w_in[l].astype(BF16), row(hgrn_norm_w[l]), row(ln_v_w[l]),
               row(ln_v_b[l]), w_a_out[l].astype(BF16), w_b_out[l].astype(BF16),
               w_o[l].astype(BF16))
    ffn_w = (row(norm_ffn_w[l]), w_up[l].astype(BF16), conv_w[l], row(conv_b[l]),
             w_down[l].astype(BF16), row(norm_ple_w[l]), w_ple_gate[l].astype(BF16),
             w_ple_proj[l].astype(BF16), row(final_norm_w))

    ws, bs = w_spatial[l], b_spatial[l]
    wmix_p, bcol_p = ws, bs[:, :, None]
    reps = LANES // seq_s
    eye = jnp.eye(reps, dtype=F32)
    wmix_s = jnp.stack([jnp.kron(eye, ws[g, :seq_s, :seq_s]) for g in range(GROUPS)])
    bcol_s = jnp.tile(bs[:, :seq_s], (1, reps))[:, :, None]

    def run(x, p, st_h, st_c, wmix, bcol, rows, chunk, ffn_rows):
        batch, seq, _ = x.shape
        outs = _mixer(x, st_h, lb_logits, mixer_w, wmix, bcol, rows=rows, chunk=chunk)
        y, conv_new = _ffn(outs[0], p.reshape(batch * seq, -1), st_c, ffn_w,
                           batch=batch, seq=seq, rows=ffn_rows)
        return (y.reshape(batch, seq, d), outs[1][None], conv_new[None]) + tuple(outs[2:])

    y_p, hs_p, cs_p = run(x_prompt, p_prompt[l], None, None, wmix_p, bcol_p, 256, 64, 256)
    y_s, hs_s, cs_s, v_s = run(x_sample, p_sample[l], state_hgrn[l], state_conv[l],
                               wmix_s, bcol_s, LANES, seq_s, 256)
    return (y_p, y_s, hs_p, hs_s, cs_p, cs_s, v_s.reshape(1, batch_s, seq_s, -1))
```

```python
import functools

import jax
import jax.numpy as jnp
from jax import lax
from jax.experimental import pallas as pl
from jax.experimental.pallas import tpu as pltpu

F32 = jnp.float32
BF16 = jnp.bfloat16
EPS = 1e-6

HEADS = 4
GROUPS = 4
CONV_W = 3
LANES = 128
SUBLANES = 8
FF_TILE = 256
VMEM_LIMIT = 56 * 1024 * 1024

NT_DIMS = (((1,), (1,)), ((), ()))
TN_DIMS = (((0,), (0,)), ((), ()))


def _rms(x, w):
    return x * lax.rsqrt(jnp.mean(x * x, axis=-1, keepdims=True) + EPS) * w


def _gelu(x):
    return 0.5 * x * (1.0 + lax.erf(x * (2.0 ** -0.5)))


def _sigmoid(x):
    return 0.5 * jnp.tanh(0.5 * x) + 0.5


def _dot(a, b):
    return jnp.dot(a, b, preferred_element_type=F32)


def _hgrn_head(q, k, b, v, st_in, chunk, scan):
    rows = q.shape[0]
    n = rows // chunk
    shift = chunk.bit_length() - 1
    b3 = b.reshape(n, chunk, LANES)

    def chunk_row(r):
        return jnp.broadcast_to(b3[:, r:r + 1, :], b3.shape).reshape(rows, LANES)

    def blocked(a):
        cols = []
        for j in range(n):
            parts = [jnp.zeros((j * chunk, LANES), F32), a[j * chunk:(j + 1) * chunk],
                     jnp.zeros((rows - (j + 1) * chunk, LANES), F32)]
            cols.append(jnp.concatenate([p for p in parts if p.shape[0]], axis=0))
        return jnp.concatenate(cols, axis=1).astype(BF16)

    b_mid = chunk_row(chunk // 2 - 1)
    b_end = chunk_row(chunk - 1)
    q_t = (q * jnp.exp(b - b_mid)).astype(BF16)
    k_t = (k * jnp.exp(b_mid - b)).astype(BF16)
    vb = v.astype(BF16)
    row = lax.broadcasted_iota(jnp.int32, (rows, rows), 0)
    col = lax.broadcasted_iota(jnp.int32, (rows, rows), 1)
    keep = ((row >> shift) == (col >> shift)) & (col <= row)
    scores = lax.dot_general(q_t, k_t, NT_DIMS, preferred_element_type=F32)
    scores = jnp.where(keep, scores, 0.0).astype(BF16)
    ds = lax.dot_general(vb, blocked(k * jnp.exp(b_end - b)), TN_DIMS, preferred_element_type=F32)
    before, after = [], []
    st = st_in
    for j in range(n):
        st_j = st if scan else st_in[j]
        before.append(st_j.astype(BF16))
        st = st_j * jnp.exp(b3[j, chunk - 1:chunk, :]) + ds[:, j * LANES:(j + 1) * LANES]
        after.append(st)
    o = _dot(scores, vb) + lax.dot_general(blocked(q * jnp.exp(b)), jnp.concatenate(before, axis=1),
                                           NT_DIMS, preferred_element_type=F32)
    return o, after


def _mixer_kernel(*refs, sample, rows, chunk, n_t):
    if sample:
        (x_ref, lbl_ref, nmw_ref, win_ref, hnw_ref, lvw_ref, lvb_ref, wmix_ref, bcol_ref,
         wa_ref, wb_ref, wo_ref, sin_ref, h_ref, sout_ref, vout_ref) = refs
    else:
        (x_ref, lbl_ref, nmw_ref, win_ref, hnw_ref, lvw_ref, lvb_ref, wmix_ref, bcol_ref,
         wa_ref, wb_ref, wo_ref, h_ref, sout_ref, st_s) = refs
    fdim = HEADS * LANES
    gw = GROUPS * LANES
    d_model = x_ref.shape[1]
    offs = [0]
    for width in (fdim, fdim, fdim, fdim, gw, gw, d_model, d_model):
        offs.append(offs[-1] + width)

    if not sample:
        @pl.when(pl.program_id(1) == 0)
        def _():
            st_s[...] = jnp.zeros_like(st_s)

    x = x_ref[...]
    xn = _rms(x, nmw_ref[...]).astype(BF16)

    def proj(i):
        return _dot(xn, win_ref[:, offs[i]:offs[i + 1]])

    lbl = lbl_ref[...]
    lbe = jnp.exp(lbl - jnp.max(lbl, axis=0, keepdims=True))
    lb = lbe[0:1, :] / jnp.sum(lbe, axis=0, keepdims=True)

    zq = proj(0)
    q = zq * _sigmoid(zq)
    fg = lb + (1.0 - lb) * _sigmoid(proj(1))
    k = 1.0 - fg
    v_in = proj(2)
    r128 = lax.broadcasted_iota(jnp.int32, (LANES, LANES), 0)
    c128 = lax.broadcasted_iota(jnp.int32, (LANES, LANES), 1)
    shift = min(chunk, LANES).bit_length() - 1
    seg = jnp.where(((r128 >> shift) == (c128 >> shift)) & (c128 <= r128), 1.0, 0.0).astype(BF16)
    log_f = jnp.log(fg)
    lf_hi = log_f.astype(BF16)
    lf_rem = log_f - lf_hi.astype(F32)
    lf_mid = lf_rem.astype(BF16)
    lf_lo = (lf_rem - lf_mid.astype(F32)).astype(BF16)
    lf3 = jnp.concatenate([lf_hi, lf_mid, lf_lo], axis=1)
    b_rows = []
    for g in range(rows // LANES):
        b3 = _dot(seg, lf3[g * LANES:(g + 1) * LANES, :])
        b_rows.append(b3[:, :fdim] + b3[:, fdim:2 * fdim] + b3[:, 2 * fdim:])
    b = jnp.concatenate(b_rows, axis=0)

    n_chunks = rows // chunk

    def head(hd):
        cols = slice(hd * LANES, (hd + 1) * LANES)
        st_in = [sin_ref[j, hd].T for j in range(n_chunks)] if sample else st_s[hd]
        o, st_out = _hgrn_head(q[:, cols], k[:, cols], b[:, cols], v_in[:, cols], st_in, chunk,
                               scan=not sample)
        if sample:
            for j in range(n_chunks):
                sout_ref[j, hd] = st_out[j].T
        else:
            st_s[hd] = st_out[-1]
        return o

    o_heads = [head(0)]
    og = _sigmoid(proj(3))
    o_heads.append(head(1))
    u = _gelu(proj(4))
    o_heads.append(head(2))
    gv = _gelu(proj(5))
    gc = gv - jnp.mean(gv, axis=-1, keepdims=True)
    v = gc * lax.rsqrt(jnp.mean(gc * gc, axis=-1, keepdims=True) + EPS) * lvw_ref[...] + lvb_ref[...]
    o_heads.append(head(3))
    gate_a = _sigmoid(proj(6))
    gate_b = _sigmoid(proj(7))

    o_a = _rms(jnp.concatenate(o_heads, axis=1) * og, hnw_ref[...]).astype(BF16)

    if sample:
        vout_ref[...] = v
    vb = v.astype(BF16)
    wm = [jnp.where(c128 <= r128, wmix_ref[g], 0.0).astype(BF16) for g in range(GROUPS)]
    bcol = [bcol_ref[g] for g in range(GROUPS)]
    n_rg = rows // LANES
    mixed_g = [
        _dot(wm[g], jnp.concatenate(
            [vb[rg * LANES:(rg + 1) * LANES, g * LANES:(g + 1) * LANES] for rg in range(n_rg)],
            axis=1)) for g in range(GROUPS)]
    mixed = jnp.concatenate(
        [jnp.concatenate([mixed_g[g][:, rg * LANES:(rg + 1) * LANES] + bcol[g]
                          for g in range(GROUPS)], axis=1) for rg in range(n_rg)], axis=0)
    o_b = (u * mixed).astype(BF16)

    mix = (gate_a * _dot(o_a, wa_ref[...]) + gate_b * _dot(o_b, wb_ref[...])).astype(BF16)
    h_ref[...] = x + _dot(mix, wo_ref[...])

    if not sample:
        @pl.when(pl.program_id(1) == n_t - 1)
        def _():
            for hd in range(HEADS):
                sout_ref[0, hd] = st_s[hd].T


def _interleave(streams, lag):
    done = [False] * len(streams)
    slot = 0
    while not all(done):
        for s, gen in enumerate(streams):
            if not done[s] and slot >= s * lag:
                try:
                    next(gen)
                except StopIteration:
                    done[s] = True
        slot += 1


def _ffn_kernel(*refs, sample, rows, seq, lag):
    if sample:
        (h_ref, p_ref, nfw_ref, wup_ref, cw_ref, cb_ref, wdn_ref, npw_ref, wpg_ref, wpp_ref,
         fnw_ref, cst_ref, y_ref, cout_ref) = refs
    else:
        (h_ref, p_ref, nfw_ref, wup_ref, cw_ref, cb_ref, wdn_ref, npw_ref, wpg_ref, wpp_ref,
         fnw_ref, y_ref, cout_ref, carry_s) = refs
        @pl.when(pl.program_id(1) == 0)
        def _():
            carry_s[...] = jnp.zeros_like(carry_s)
    d_ff = wdn_ref.shape[0]
    n_seq = rows // seq
    n_tiles = d_ff // FF_TILE
    rid = lax.broadcasted_iota(jnp.int32, (rows, FF_TILE), 0)
    tpos = (rid & (seq - 1)) if sample else rid

    def stream(s):
        h = h_ref[s]
        hn = _rms(h, nfw_ref[...]).astype(BF16)

        def conv(up, lo):
            cols = slice(lo, lo + FF_TILE)
            w = cw_ref[:, cols]
            if sample:
                st = cst_ref[s, :, :, cols]
                e0 = jnp.broadcast_to(st[:, 0:1, :], (n_seq, seq, FF_TILE)).reshape(rows, FF_TILE)
                e1 = jnp.broadcast_to(st[:, 1:2, :], (n_seq, seq, FF_TILE)).reshape(rows, FF_TILE)
                cout_ref[s, :, :, cols] = up.reshape(n_seq, seq, FF_TILE)[:, seq - 2:seq, :]
            else:
                e0 = carry_s[s, 0:1, cols]
                e1 = carry_s[s, 1:2, cols]
                carry_s[s, :, cols] = up[rows - 2:rows, :]
                cout_ref[s, 0, :, cols] = up[rows - 2:rows, :]
            x1 = jnp.where(tpos >= 1, pltpu.roll(up, 1, 0), e1)
            x2 = jnp.where(tpos >= 2, pltpu.roll(up, 2, 0), jnp.where(tpos == 1, e1, e0))
            return cb_ref[:, cols] + x2 * w[0:1, :] + x1 * w[1:2, :] + up * w[2:3, :]

        def up_pair(j):
            lo_a = j * FF_TILE
            lo_b = d_ff + j * FF_TILE
            return (_dot(hn, wup_ref[:, lo_a:lo_a + FF_TILE]),
                    _dot(hn, wup_ref[:, lo_b:lo_b + FF_TILE]))

        acc = jnp.zeros((rows, h.shape[1]), F32)
        nxt = up_pair(0)
        pp = _dot(p_ref[s].astype(BF16), wpp_ref[...])
        gated = None
        yield
        for j in range(n_tiles + 1):
            lo_a = j * FF_TILE
            cur = nxt
            if j + 1 < n_tiles:
                nxt = up_pair(j + 1)
            if j < n_tiles:
                ya = conv(cur[0], lo_a)
                yb = conv(cur[1], d_ff + lo_a)
                new_gated = (_gelu(ya) * yb).astype(BF16)
            if j > 0:
                acc = acc + _dot(gated, wdn_ref[lo_a - FF_TILE:lo_a, :])
            gated = new_gated
            yield
        h2 = h + acc
        inv = lax.rsqrt(jnp.mean(h2 * h2, axis=-1, keepdims=True) + EPS)
        gate = _sigmoid(inv * _dot((h2 * npw_ref[...]).astype(BF16), wpg_ref[...]))
        h3 = h2 + gate * pp
        y_ref[s] = _rms(h3, fnw_ref[...])

    _interleave([stream(s) for s in range(h_ref.shape[0])], lag)


def _const_spec(shape):
    nd = len(shape)
    return pl.BlockSpec(shape, lambda *_: (0,) * nd, pipeline_mode=pl.Buffered(1))


def _mixer(x, state, lb_logits, weights, wmix, bcol, *, rows, chunk):
    batch, seq, d = x.shape
    sample = state is not None
    x2 = x.reshape(batch * seq, d)
    fdim = HEADS * LANES
    const_args = (lb_logits,) + tuple(weights[:5]) + (wmix, bcol) + tuple(weights[5:])
    const_specs = [_const_spec(a.shape) for a in const_args]
    scratch = []
    if sample:
        n_seq = rows // seq
        grid = (batch // n_seq,)
        row_map = lambda i: (i, 0)
        st_spec = pl.BlockSpec((n_seq, HEADS, LANES, LANES), lambda i: (i, 0, 0, 0))
        in_specs = [pl.BlockSpec((rows, d), row_map)] + const_specs + [st_spec]
        out_specs = [pl.BlockSpec((rows, d), row_map), st_spec,
                     pl.BlockSpec((rows, GROUPS * LANES), row_map)]
        out_shape = [jax.ShapeDtypeStruct((batch * seq, d), F32),
                     jax.ShapeDtypeStruct(state.shape, F32),
                     jax.ShapeDtypeStruct((batch * seq, GROUPS * LANES), F32)]
        args = (x2,) + const_args + (state,)
        n_t = 1
        sem = ("arbitrary",)
    else:
        n_t = seq // rows
        grid = (batch, n_t)
        row_map = lambda b, t: (b * n_t + t, 0)
        in_specs = [pl.BlockSpec((rows, d), row_map)] + const_specs
        out_specs = [pl.BlockSpec((rows, d), row_map),
                     pl.BlockSpec((1, HEADS, LANES, LANES), lambda b, t: (b, 0, 0, 0))]
        out_shape = [jax.ShapeDtypeStruct((batch * seq, d), F32),
                     jax.ShapeDtypeStruct((batch, HEADS, LANES, LANES), F32)]
        scratch = [pltpu.VMEM((HEADS, LANES, LANES), F32)]
        args = (x2,) + const_args
        sem = ("arbitrary", "arbitrary")
    return pl.pallas_call(
        functools.partial(_mixer_kernel, sample=sample, rows=rows, chunk=chunk, n_t=n_t),
        grid=grid, in_specs=in_specs, out_specs=out_specs, out_shape=out_shape,
        scratch_shapes=scratch,
        compiler_params=pltpu.CompilerParams(dimension_semantics=sem,
                                             vmem_limit_bytes=VMEM_LIMIT),
        name="mixer_sample" if sample else "mixer_prompt",
    )(*args)


def _ffn(h2d, p2d, conv_state, weights, *, batch, seq, rows, streams):
    d = h2d.shape[1]
    sample = conv_state is not None
    const_specs = [_const_spec(a.shape) for a in weights]
    width = weights[2].shape[1]
    sb = batch // streams
    h3d = h2d.reshape(streams, sb * seq, d)
    p3d = p2d.reshape(streams, sb * seq, -1)
    if sample:
        n_seq = rows // seq
        grid = (sb // n_seq,)
        row_map = lambda i: (0, i, 0)
        cs_spec = pl.BlockSpec((streams, n_seq, CONV_W - 1, width), lambda i: (0, i, 0, 0))
        in_specs = ([pl.BlockSpec((streams, rows, d), row_map),
                     pl.BlockSpec((streams, rows, p3d.shape[2]), row_map)]
                    + const_specs + [cs_spec])
        out_specs = [pl.BlockSpec((streams, rows, d), row_map), cs_spec]
        args = (h3d, p3d) + tuple(weights) + (conv_state.reshape(streams, sb, CONV_W - 1, width),)
        scratch = []
        sem = ("arbitrary",)
    else:
        n_t = seq // rows
        grid = (sb, n_t)
        row_map = lambda b, t: (0, b * n_t + t, 0)
        in_specs = ([pl.BlockSpec((streams, rows, d), row_map),
                     pl.BlockSpec((streams, rows, p3d.shape[2]), row_map)]
                    + const_specs)
        out_specs = [pl.BlockSpec((streams, rows, d), row_map),
                     pl.BlockSpec((streams, 1, CONV_W - 1, width), lambda b, t: (0, b, 0, 0))]
        args = (h3d, p3d) + tuple(weights)
        scratch = [pltpu.VMEM((streams, CONV_W - 1, width), F32)]
        sem = ("arbitrary", "arbitrary")
    out_shape = [jax.ShapeDtypeStruct((streams, sb * seq, d), F32),
                 jax.ShapeDtypeStruct((streams, sb, CONV_W - 1, width), F32)]
    n_slots = weights[4].shape[0] // FF_TILE + 3
    y, conv_new = pl.pallas_call(
        functools.partial(_ffn_kernel, sample=sample, rows=rows, seq=seq, lag=n_slots // 2),
        grid=grid, in_specs=in_specs, out_specs=out_specs, out_shape=out_shape,
        scratch_shapes=scratch,
        compiler_params=pltpu.CompilerParams(dimension_semantics=sem,
                                             vmem_limit_bytes=VMEM_LIMIT),
        name="ffn_sample" if sample else "ffn_prompt",
    )(*args)
    return y.reshape(batch * seq, d), conv_new.reshape(batch, CONV_W - 1, width)


def kernel(x_prompt, x_sample, p_prompt, p_sample, state_hgrn, state_conv, lb_logits, norm_mix_w, w_in, hgrn_norm_w, ln_v_w, ln_v_b, w_spatial, b_spatial, w_a_out, w_b_out, w_o, norm_ffn_w, w_up, conv_w, conv_b, w_down, norm_ple_w, w_ple_gate, w_ple_proj, final_norm_w):
    depth = w_in.shape[0]
    assert depth == 1 and lb_logits.shape[0] == 2
    batch_p, seq_p, d = x_prompt.shape
    batch_s, seq_s, _ = x_sample.shape
    assert seq_s == SUBLANES and LANES % seq_s == 0
    l = 0
    row = lambda a: a.reshape(1, -1)
    mixer_w = (row(norm_mix_w[l]), w_in[l].astype(BF16), row(hgrn_norm_w[l]), row(ln_v_w[l]),
               row(ln_v_b[l]), w_a_out[l].astype(BF16), w_b_out[l].astype(BF16),
               w_o[l].astype(BF16))
    ffn_w = (row(norm_ffn_w[l]), w_up[l].astype(BF16), conv_w[l], row(conv_b[l]),
             w_down[l].astype(BF16), row(norm_ple_w[l]), w_ple_gate[l].astype(BF16),
             w_ple_proj[l].astype(BF16), row(final_norm_w))

    ws, bs = w_spatial[l], b_spatial[l]
    wmix_p, bcol_p = ws, bs[:, :, None]
    reps = LANES // seq_s
    eye = jnp.eye(reps, dtype=F32)
    wmix_s = jnp.stack([jnp.kron(eye, ws[g, :seq_s, :seq_s]) for g in range(GROUPS)])
    bcol_s = jnp.tile(bs[:, :seq_s], (1, reps))[:, :, None]

    def run(x, p, st_h, st_c, wmix, bcol, rows, chunk, ffn_rows):
        batch, seq, _ = x.shape
        outs = _mixer(x, st_h, lb_logits, mixer_w, wmix, bcol, rows=rows, chunk=chunk)
        y, conv_new = _ffn(outs[0], p.reshape(batch * seq, -1), st_c, ffn_w,
                           batch=batch, seq=seq, rows=ffn_rows[0], streams=ffn_rows[1])
        return (y.reshape(batch, seq, d), outs[1][None], conv_new[None]) + tuple(outs[2:])

    y_p, hs_p, cs_p = run(x_prompt, p_prompt[l], None, None, wmix_p, bcol_p, 256, 64, (256, 2))
    y_s, hs_s, cs_s, v_s = run(x_sample, p_sample[l], state_hgrn[l], state_conv[l],
                               wmix_s, bcol_s, LANES, seq_s, (256, 1))
    return (y_p, y_s, hs_p, hs_s, cs_p, cs_s, v_s.reshape(1, batch_s, seq_s, -1))
```

```python
import functools

import jax
import jax.numpy as jnp
from jax import lax
from jax.experimental import pallas as pl
from jax.experimental.pallas import tpu as pltpu

F32 = jnp.float32
BF16 = jnp.bfloat16
EPS = 1e-6

HEADS = 4
GROUPS = 4
CONV_W = 3
LANES = 128
SUBLANES = 8
FF_TILE = 256
VMEM_LIMIT = 56 * 1024 * 1024

NT_DIMS = (((1,), (1,)), ((), ()))
TN_DIMS = (((0,), (0,)), ((), ()))


def _rms(x, w):
    return x * lax.rsqrt(jnp.mean(x * x, axis=-1, keepdims=True) + EPS) * w


def _gelu(x):
    return 0.5 * x * (1.0 + lax.erf(x * (2.0 ** -0.5)))


def _dot(a, b):
    return jnp.dot(a, b, preferred_element_type=F32)


def _hgrn_head(q, k, b, v, st_in, chunk, scan):
    rows = q.shape[0]
    n = rows // chunk
    shift = chunk.bit_length() - 1
    b3 = b.reshape(n, chunk, LANES)

    def chunk_row(r):
        return jnp.broadcast_to(b3[:, r:r + 1, :], b3.shape).reshape(rows, LANES)

    def blocked(a):
        cols = []
        for j in range(n):
            parts = [jnp.zeros((j * chunk, LANES), F32), a[j * chunk:(j + 1) * chunk],
                     jnp.zeros((rows - (j + 1) * chunk, LANES), F32)]
            cols.append(jnp.concatenate([p for p in parts if p.shape[0]], axis=0))
        return jnp.concatenate(cols, axis=1).astype(BF16)

    b_mid = chunk_row(chunk // 2 - 1)
    b_end = chunk_row(chunk - 1)
    q_t = (q * jnp.exp(b - b_mid)).astype(BF16)
    k_t = (k * jnp.exp(b_mid - b)).astype(BF16)
    vb = v.astype(BF16)
    k_blk = blocked(k * jnp.exp(b_end - b))
    q_blk = blocked(q * jnp.exp(b))
    yield
    scores = lax.dot_general(q_t, k_t, NT_DIMS, preferred_element_type=F32)
    ds = lax.dot_general(vb, k_blk, TN_DIMS, preferred_element_type=F32)
    yield
    row = lax.broadcasted_iota(jnp.int32, (rows, rows), 0)
    col = lax.broadcasted_iota(jnp.int32, (rows, rows), 1)
    keep = ((row >> shift) == (col >> shift)) & (col <= row)
    scores = jnp.where(keep, scores, 0.0).astype(BF16)
    before, after = [], []
    st = st_in
    for j in range(n):
        st_j = st if scan else st_in[j]
        before.append(st_j.astype(BF16))
        st = st_j * jnp.exp(b3[j, chunk - 1:chunk, :]) + ds[:, j * LANES:(j + 1) * LANES]
        after.append(st)
    s_rhs = jnp.concatenate(before, axis=1)
    yield
    o = _dot(scores, vb) + lax.dot_general(q_blk, s_rhs, NT_DIMS, preferred_element_type=F32)
    return o, after


def _interleave(streams, steps):
    done = [False] * len(streams)
    while not all(done):
        for s, gen in enumerate(streams):
            for _ in range(steps[s]):
                if not done[s]:
                    try:
                        next(gen)
                    except StopIteration:
                        done[s] = True


def _mixer_stream(x, wrefs, state_in, state_out, emit, *, rows, chunk, scan):
    (lbl_ref, nmw_ref, win_ref, hnw_ref, lvw_ref, lvb_ref, wmix_ref, bcol_ref,
     wa_ref, wb_ref, wo_ref) = wrefs
    fdim = HEADS * LANES
    gw = GROUPS * LANES
    d_model = x.shape[1]
    offs = [0]
    for width in (fdim, fdim, fdim, fdim, gw, gw, d_model, d_model):
        offs.append(offs[-1] + width)

    xn = _rms(x, nmw_ref[...]).astype(BF16)

    def proj(i):
        return _dot(xn, win_ref[:, offs[i]:offs[i + 1]])

    lbl = lbl_ref[...]
    lbe = jnp.exp(lbl - jnp.max(lbl, axis=0, keepdims=True))
    lb = lbe[0:1, :] / jnp.sum(lbe, axis=0, keepdims=True)

    fg = lb + (1.0 - lb) * jax.nn.sigmoid(proj(1))
    k = 1.0 - fg
    log_f = jnp.log(fg)
    lf_hi = log_f.astype(BF16)
    lf_rem = log_f - lf_hi.astype(F32)
    lf_mid = lf_rem.astype(BF16)
    lf_lo = (lf_rem - lf_mid.astype(F32)).astype(BF16)
    lf3 = jnp.concatenate([lf_hi, lf_mid, lf_lo], axis=1)
    yield
    q = jax.nn.silu(proj(0))
    yield
    v_in = proj(2)
    yield
    r128 = lax.broadcasted_iota(jnp.int32, (LANES, LANES), 0)
    c128 = lax.broadcasted_iota(jnp.int32, (LANES, LANES), 1)
    shift = min(chunk, LANES).bit_length() - 1
    seg = jnp.where(((r128 >> shift) == (c128 >> shift)) & (c128 <= r128), 1.0, 0.0).astype(BF16)
    b_rows = []
    for g in range(rows // LANES):
        b3 = _dot(seg, lf3[g * LANES:(g + 1) * LANES, :])
        b_rows.append(b3[:, :fdim] + b3[:, fdim:2 * fdim] + b3[:, 2 * fdim:])
    b = jnp.concatenate(b_rows, axis=0)
    yield

    def head(hd):
        cols = slice(hd * LANES, (hd + 1) * LANES)
        o, st_out = yield from _hgrn_head(q[:, cols], k[:, cols], b[:, cols], v_in[:, cols],
                                          state_in(hd), chunk, scan)
        state_out(hd, st_out)
        return o

    o_heads = [(yield from head(0))]
    yield
    og = jax.nn.sigmoid(proj(3))
    yield
    o_heads.append((yield from head(1)))
    yield
    u = _gelu(proj(4))
    yield
    o_heads.append((yield from head(2)))
    yield
    gv = _gelu(proj(5))
    gc = gv - jnp.mean(gv, axis=-1, keepdims=True)
    v = gc * lax.rsqrt(jnp.mean(gc * gc, axis=-1, keepdims=True) + EPS) * lvw_ref[...] + lvb_ref[...]
    emit("v", v)
    yield
    o_heads.append((yield from head(3)))
    yield
    gate_a = jax.nn.sigmoid(proj(6))
    yield
    gate_b = jax.nn.sigmoid(proj(7))
    yield
    o_a = _rms(jnp.concatenate(o_heads, axis=1) * og, hnw_ref[...]).astype(BF16)

    vb = v.astype(BF16)
    wm = [jnp.where(c128 <= r128, wmix_ref[g], 0.0).astype(BF16) for g in range(GROUPS)]
    bcol = [bcol_ref[g] for g in range(GROUPS)]
    n_rg = rows // LANES
    mixed_g = [
        _dot(wm[g], jnp.concatenate(
            [vb[rg * LANES:(rg + 1) * LANES, g * LANES:(g + 1) * LANES] for rg in range(n_rg)],
            axis=1)) for g in range(GROUPS)]
    a_out = _dot(o_a, wa_ref[...])
    yield
    mixed = jnp.concatenate(
        [jnp.concatenate([mixed_g[g][:, rg * LANES:(rg + 1) * LANES] + bcol[g]
                          for g in range(GROUPS)], axis=1) for rg in range(n_rg)], axis=0)
    o_b = (u * mixed).astype(BF16)
    yield

    mix = (gate_a * a_out + gate_b * _dot(o_b, wb_ref[...])).astype(BF16)
    yield
    emit("h", x + _dot(mix, wo_ref[...]))


def _ffn_stream(h, p, wrefs, prev_rows, keep_rows, emit, *, rows, tpos):
    (nfw_ref, wup_ref, cw_ref, cb_ref, wdn_ref, npw_ref, wpg_ref, wpp_ref, fnw_ref) = wrefs
    d_ff = wdn_ref.shape[0]
    n_tiles = d_ff // FF_TILE
    hn = _rms(h, nfw_ref[...]).astype(BF16)

    def conv(up, lo):
        cols = slice(lo, lo + FF_TILE)
        w = cw_ref[:, cols]
        e0, e1 = prev_rows(cols)
        keep_rows(cols, up)
        x1 = jnp.where(tpos >= 1, pltpu.roll(up, 1, 0), e1)
        x2 = jnp.where(tpos >= 2, pltpu.roll(up, 2, 0), jnp.where(tpos == 1, e1, e0))
        return cb_ref[:, cols] + x2 * w[0:1, :] + x1 * w[1:2, :] + up * w[2:3, :]

    def up_pair(j):
        lo_a = j * FF_TILE
        lo_b = d_ff + j * FF_TILE
        return (_dot(hn, wup_ref[:, lo_a:lo_a + FF_TILE]), _dot(hn, wup_ref[:, lo_b:lo_b + FF_TILE]))

    acc = jnp.zeros((rows, h.shape[1]), F32)
    nxt = up_pair(0)
    pp = _dot(p.astype(BF16), wpp_ref[...])
    gated = None
    yield
    for j in range(n_tiles + 1):
        lo_a = j * FF_TILE
        cur = nxt
        if j + 1 < n_tiles:
            nxt = up_pair(j + 1)
        if j < n_tiles:
            ya = conv(cur[0], lo_a)
            yb = conv(cur[1], d_ff + lo_a)
            new_gated = (_gelu(ya) * yb).astype(BF16)
        if j > 0:
            acc = acc + _dot(gated, wdn_ref[lo_a - FF_TILE:lo_a, :])
        gated = new_gated
        yield
    h2 = h + acc
    gate = jax.nn.sigmoid(_dot(_rms(h2, npw_ref[...]).astype(BF16), wpg_ref[...]))
    yield
    h3 = h2 + gate * pp
    emit("y", _rms(h3, fnw_ref[...]))


def _prompt_kernel(*refs, rows, chunk, n_t, n_blocks):
    x_ref, p_ref = refs[:2]
    mixer_w = refs[2:13]
    ffn_w = refs[13:22]
    y_ref, sout_ref, cout_ref, st_s, carry_s, h_s = refs[22:]
    i = pl.program_id(0)
    t_mix = lax.rem(jnp.minimum(i, n_blocks - 1), n_t)
    t_ffn = lax.rem(jnp.maximum(i - 1, 0), n_t)

    @pl.when(i == 0)
    def _():
        h_s[...] = jnp.zeros_like(h_s)

    @pl.when(t_mix == 0)
    def _():
        st_s[...] = jnp.zeros_like(st_s)

    @pl.when(t_ffn == 0)
    def _():
        carry_s[...] = jnp.zeros_like(carry_s)

    slot = lax.rem(i, 2)
    results = {}

    def state_out(hd, states):
        st_s[hd] = states[-1]

    def prev_rows(cols):
        return carry_s[0:1, cols], carry_s[1:2, cols]

    def keep_rows(cols, up):
        carry_s[:, cols] = up[rows - 2:rows, :]
        cout_ref[0, :, cols] = up[rows - 2:rows, :]

    tpos = lax.broadcasted_iota(jnp.int32, (rows, FF_TILE), 0)
    _interleave([
        _ffn_stream(h_s[1 - slot], p_ref[...], ffn_w, prev_rows, keep_rows, results.__setitem__,
                    rows=rows, tpos=tpos),
        _mixer_stream(x_ref[...], mixer_w, lambda hd: st_s[hd], state_out, results.__setitem__,
                      rows=rows, chunk=chunk, scan=True),
    ], steps=(1, 2))
    y_ref[...] = results["y"]
    h_s[slot] = results["h"]

    @pl.when((t_mix == n_t - 1) & (i < n_blocks))
    def _():
        for hd in range(HEADS):
            sout_ref[0, hd] = st_s[hd].T


def _mixer_sample_kernel(*refs, rows, seq):
    x_ref = refs[0]
    wrefs = refs[1:12]
    sin_ref, h_ref, sout_ref, vout_ref = refs[12:]
    n_seq = rows // seq
    results = {}

    def state_out(hd, states):
        for j in range(n_seq):
            sout_ref[j, hd] = states[j].T

    for _ in _mixer_stream(x_ref[...], wrefs,
                           lambda hd: [sin_ref[j, hd].T for j in range(n_seq)], state_out,
                           results.__setitem__, rows=rows, chunk=seq, scan=False):
        pass
    vout_ref[...] = results["v"]
    h_ref[...] = results["h"]


def _ffn_sample_kernel(*refs, rows, seq):
    h_ref, p_ref = refs[:2]
    wrefs = refs[2:11]
    cst_ref, y_ref, cout_ref = refs[11:]
    n_seq = rows // seq
    results = {}

    def prev_rows(cols):
        st = cst_ref[:, :, cols]
        width = st.shape[-1]
        return tuple(jnp.broadcast_to(st[:, r:r + 1, :], (n_seq, seq, width)).reshape(rows, width)
                     for r in range(CONV_W - 1))

    def keep_rows(cols, up):
        cout_ref[:, :, cols] = up.reshape(n_seq, seq, up.shape[-1])[:, seq - (CONV_W - 1):seq, :]

    tpos = lax.broadcasted_iota(jnp.int32, (rows, FF_TILE), 0) & (seq - 1)
    for _ in _ffn_stream(h_ref[...], p_ref[...], wrefs, prev_rows, keep_rows, results.__setitem__,
                         rows=rows, tpos=tpos):
        pass
    y_ref[...] = results["y"]


def _const_spec(shape):
    nd = len(shape)
    return pl.BlockSpec(shape, lambda *_: (0,) * nd, pipeline_mode=pl.Buffered(1))


def _params(n_axes):
    return pltpu.CompilerParams(dimension_semantics=("arbitrary",) * n_axes,
                                vmem_limit_bytes=VMEM_LIMIT)


def _prompt(x, p, mixer_consts, ffn_consts, *, rows, chunk):
    batch, seq, d = x.shape
    n_t = seq // rows
    n_blocks = batch * n_t
    width = ffn_consts[2].shape[1]
    mix_blk = lambda i: jnp.minimum(i, n_blocks - 1)
    ffn_blk = lambda i: jnp.maximum(i - 1, 0)
    consts = tuple(mixer_consts) + tuple(ffn_consts)
    in_specs = ([pl.BlockSpec((rows, d), lambda i: (mix_blk(i), 0)),
                 pl.BlockSpec((rows, p.shape[-1]), lambda i: (ffn_blk(i), 0))]
                + [_const_spec(a.shape) for a in consts])
    out_specs = [pl.BlockSpec((rows, d), lambda i: (ffn_blk(i), 0)),
                 pl.BlockSpec((1, HEADS, LANES, LANES), lambda i: (mix_blk(i) // n_t, 0, 0, 0)),
                 pl.BlockSpec((1, CONV_W - 1, width), lambda i: (ffn_blk(i) // n_t, 0, 0))]
    out_shape = [jax.ShapeDtypeStruct((batch * seq, d), F32),
                 jax.ShapeDtypeStruct((batch, HEADS, LANES, LANES), F32),
                 jax.ShapeDtypeStruct((batch, CONV_W - 1, width), F32)]
    scratch = [pltpu.VMEM((HEADS, LANES, LANES), F32), pltpu.VMEM((CONV_W - 1, width), F32),
               pltpu.VMEM((2, rows, d), F32)]
    y, st, cs = pl.pallas_call(
        functools.partial(_prompt_kernel, rows=rows, chunk=chunk, n_t=n_t, n_blocks=n_blocks),
        grid=(n_blocks + 1,), in_specs=in_specs, out_specs=out_specs, out_shape=out_shape,
        scratch_shapes=scratch, compiler_params=_params(1), name="prompt_step",
    )(x.reshape(batch * seq, d), p.reshape(batch * seq, -1), *consts)
    return y.reshape(batch, seq, d), st, cs


def _sample(x, p, state, conv_state, mixer_consts, ffn_consts, *, mixer_rows, ffn_rows):
    batch, seq, d = x.shape
    width = ffn_consts[2].shape[1]
    gw = GROUPS * LANES
    rows = mixer_rows
    n_seq = rows // seq
    row_map = lambda i: (i, 0)
    st_spec = pl.BlockSpec((n_seq, HEADS, LANES, LANES), lambda i: (i, 0, 0, 0))
    h, st, v = pl.pallas_call(
        functools.partial(_mixer_sample_kernel, rows=rows, seq=seq),
        grid=(batch // n_seq,),
        in_specs=([pl.BlockSpec((rows, d), row_map)] + [_const_spec(a.shape) for a in mixer_consts]
                  + [st_spec]),
        out_specs=[pl.BlockSpec((rows, d), row_map), st_spec, pl.BlockSpec((rows, gw), row_map)],
        out_shape=[jax.ShapeDtypeStruct((batch * seq, d), F32),
                   jax.ShapeDtypeStruct(state.shape, F32),
                   jax.ShapeDtypeStruct((batch * seq, gw), F32)],
        compiler_params=_params(1), name="mixer_sample",
    )(x.reshape(batch * seq, d), *mixer_consts, state)
    rows = ffn_rows
    n_seq = rows // seq
    cs_spec = pl.BlockSpec((n_seq, CONV_W - 1, width), lambda i: (i, 0, 0))
    y, cs = pl.pallas_call(
        functools.partial(_ffn_sample_kernel, rows=rows, seq=seq),
        grid=(batch // n_seq,),
        in_specs=([pl.BlockSpec((rows, d), row_map), pl.BlockSpec((rows, p.shape[-1]), row_map)]
                  + [_const_spec(a.shape) for a in ffn_consts] + [cs_spec]),
        out_specs=[pl.BlockSpec((rows, d), row_map), cs_spec],
        out_shape=[jax.ShapeDtypeStruct((batch * seq, d), F32),
                   jax.ShapeDtypeStruct(conv_state.shape, F32)],
        compiler_params=_params(1), name="ffn_sample",
    )(h, p.reshape(batch * seq, -1), *ffn_consts, conv_state)
    return y.reshape(batch, seq, d), st, cs, v.reshape(batch, seq, gw)


def kernel(x_prompt, x_sample, p_prompt, p_sample, state_hgrn, state_conv, lb_logits, norm_mix_w, w_in, hgrn_norm_w, ln_v_w, ln_v_b, w_spatial, b_spatial, w_a_out, w_b_out, w_o, norm_ffn_w, w_up, conv_w, conv_b, w_down, norm_ple_w, w_ple_gate, w_ple_proj, final_norm_w):
    depth = w_in.shape[0]
    assert depth == 1 and lb_logits.shape[0] == 2
    seq_s = x_sample.shape[1]
    assert seq_s == SUBLANES and LANES % seq_s == 0
    l = 0
    row = lambda a: a.reshape(1, -1)

    ws, bs = w_spatial[l], b_spatial[l]
    reps = LANES // seq_s
    eye = jnp.eye(reps, dtype=F32)
    wmix_s = jnp.stack([jnp.kron(eye, ws[g, :seq_s, :seq_s]) for g in range(GROUPS)])
    bcol_s = jnp.tile(bs[:, :seq_s], (1, reps))[:, :, None]

    def mixer_consts(wmix, bcol):
        return (lb_logits, row(norm_mix_w[l]), w_in[l].astype(BF16), row(hgrn_norm_w[l]),
                row(ln_v_w[l]), row(ln_v_b[l]), wmix, bcol, w_a_out[l].astype(BF16),
                w_b_out[l].astype(BF16), w_o[l].astype(BF16))

    ffn_consts = (row(norm_ffn_w[l]), w_up[l].astype(BF16), conv_w[l], row(conv_b[l]),
                  w_down[l].astype(BF16), row(norm_ple_w[l]), w_ple_gate[l].astype(BF16),
                  w_ple_proj[l].astype(BF16), row(final_norm_w))

    y_p, hs_p, cs_p = _prompt(x_prompt, p_prompt[l], mixer_consts(ws, bs[:, :, None]), ffn_consts,
                              rows=256, chunk=64)
    y_s, hs_s, cs_s, v_s = _sample(x_sample, p_sample[l], state_hgrn[l], state_conv[l],
                                   mixer_consts(wmix_s, bcol_s), ffn_consts,
                                   mixer_rows=LANES, ffn_rows=256)
    return (y_p, y_s, hs_p[None], hs_s[None], cs_p[None], cs_s[None], v_s[None])
```

```python
import functools

import jax
import jax.numpy as jnp
from jax import lax
from jax.experimental import pallas as pl
from jax.experimental.pallas import tpu as pltpu

F32 = jnp.float32
BF16 = jnp.bfloat16
EPS = 1e-6

HEADS = 4
GROUPS = 4
CONV_W = 3
LANES = 128
SUBLANES = 8
FF_TILE = 256
VMEM_LIMIT = 56 * 1024 * 1024

NT_DIMS = (((1,), (1,)), ((), ()))
TN_DIMS = (((0,), (0,)), ((), ()))


def _rms(x, w):
    return x * lax.rsqrt(jnp.mean(x * x, axis=-1, keepdims=True) + EPS) * w


def _gelu(x):
    return 0.5 * x * (1.0 + lax.erf(x * (2.0 ** -0.5)))


def _dot(a, b):
    return jnp.dot(a, b, preferred_element_type=F32)


def _hgrn_head(q, k, b, v, st_in, chunk, scan):
    rows = q.shape[0]
    n = rows // chunk
    shift = chunk.bit_length() - 1
    b3 = b.reshape(n, chunk, LANES)

    def chunk_row(r):
        return jnp.broadcast_to(b3[:, r:r + 1, :], b3.shape).reshape(rows, LANES)

    def blocked(a):
        cols = []
        for j in range(n):
            parts = [jnp.zeros((j * chunk, LANES), F32), a[j * chunk:(j + 1) * chunk],
                     jnp.zeros((rows - (j + 1) * chunk, LANES), F32)]
            cols.append(jnp.concatenate([p for p in parts if p.shape[0]], axis=0))
        return jnp.concatenate(cols, axis=1).astype(BF16)

    b_mid = chunk_row(chunk // 2 - 1)
    b_end = chunk_row(chunk - 1)
    q_t = (q * jnp.exp(b - b_mid)).astype(BF16)
    k_t = (k * jnp.exp(b_mid - b)).astype(BF16)
    vb = v.astype(BF16)
    k_blk = blocked(k * jnp.exp(b_end - b))
    q_blk = blocked(q * jnp.exp(b))
    yield
    scores = lax.dot_general(q_t, k_t, NT_DIMS, preferred_element_type=F32)
    ds = lax.dot_general(vb, k_blk, TN_DIMS, preferred_element_type=F32)
    yield
    row = lax.broadcasted_iota(jnp.int32, (rows, rows), 0)
    col = lax.broadcasted_iota(jnp.int32, (rows, rows), 1)
    keep = ((row >> shift) == (col >> shift)) & (col <= row)
    scores = jnp.where(keep, scores, 0.0).astype(BF16)
    before, after = [], []
    st = st_in
    for j in range(n):
        st_j = st if scan else st_in[j]
        before.append(st_j.astype(BF16))
        st = st_j * jnp.exp(b3[j, chunk - 1:chunk, :]) + ds[:, j * LANES:(j + 1) * LANES]
        after.append(st)
    s_rhs = jnp.concatenate(before, axis=1)
    yield
    o = _dot(scores, vb) + lax.dot_general(q_blk, s_rhs, NT_DIMS, preferred_element_type=F32)
    return o, after


def _interleave(streams, steps, lead=0):
    done = [False] * len(streams)
    for _ in range(lead):
        next(streams[0])
    while not all(done):
        for s, gen in enumerate(streams):
            for _ in range(steps[s]):
                if not done[s]:
                    try:
                        next(gen)
                    except StopIteration:
                        done[s] = True


def _mixer_stream(x, wrefs, state_in, state_out, emit, *, rows, chunk, scan):
    (lbl_ref, nmw_ref, win_ref, hnw_ref, lvw_ref, lvb_ref, wmix_ref, bcol_ref,
     wa_ref, wb_ref, wo_ref) = wrefs
    fdim = HEADS * LANES
    gw = GROUPS * LANES
    d_model = x.shape[1]
    offs = [0]
    for width in (fdim, fdim, fdim, fdim, gw, gw, d_model, d_model):
        offs.append(offs[-1] + width)

    xn = _rms(x, nmw_ref[...]).astype(BF16)

    def proj(i):
        return _dot(xn, win_ref[:, offs[i]:offs[i + 1]])

    lbl = lbl_ref[...]
    lbe = jnp.exp(lbl - jnp.max(lbl, axis=0, keepdims=True))
    lb = lbe[0:1, :] / jnp.sum(lbe, axis=0, keepdims=True)

    fg = lb + (1.0 - lb) * jax.nn.sigmoid(proj(1))
    k = 1.0 - fg
    log_f = jnp.log(fg)
    lf_hi = log_f.astype(BF16)
    lf_rem = log_f - lf_hi.astype(F32)
    lf_mid = lf_rem.astype(BF16)
    lf_lo = (lf_rem - lf_mid.astype(F32)).astype(BF16)
    lf3 = jnp.concatenate([lf_hi, lf_mid, lf_lo], axis=1)
    yield
    q = jax.nn.silu(proj(0))
    yield
    v_in = proj(2)
    yield
    r128 = lax.broadcasted_iota(jnp.int32, (LANES, LANES), 0)
    c128 = lax.broadcasted_iota(jnp.int32, (LANES, LANES), 1)
    shift = min(chunk, LANES).bit_length() - 1
    seg = jnp.where(((r128 >> shift) == (c128 >> shift)) & (c128 <= r128), 1.0, 0.0).astype(BF16)
    b_rows = []
    for g in range(rows // LANES):
        b3 = _dot(seg, lf3[g * LANES:(g + 1) * LANES, :])
        b_rows.append(b3[:, :fdim] + b3[:, fdim:2 * fdim] + b3[:, 2 * fdim:])
    b = jnp.concatenate(b_rows, axis=0)
    yield

    def head(hd):
        cols = slice(hd * LANES, (hd + 1) * LANES)
        o, st_out = yield from _hgrn_head(q[:, cols], k[:, cols], b[:, cols], v_in[:, cols],
                                          state_in(hd), chunk, scan)
        state_out(hd, st_out)
        return o

    o_heads = [(yield from head(0))]
    yield
    og = jax.nn.sigmoid(proj(3))
    yield
    o_heads.append((yield from head(1)))
    yield
    u = _gelu(proj(4))
    yield
    o_heads.append((yield from head(2)))
    yield
    gv = _gelu(proj(5))
    gc = gv - jnp.mean(gv, axis=-1, keepdims=True)
    v = gc * lax.rsqrt(jnp.mean(gc * gc, axis=-1, keepdims=True) + EPS) * lvw_ref[...] + lvb_ref[...]
    emit("v", v)
    yield
    o_heads.append((yield from head(3)))
    yield
    gate_a = jax.nn.sigmoid(proj(6))
    yield
    gate_b = jax.nn.sigmoid(proj(7))
    yield
    o_a = _rms(jnp.concatenate(o_heads, axis=1) * og, hnw_ref[...]).astype(BF16)

    vb = v.astype(BF16)
    wm = [jnp.where(c128 <= r128, wmix_ref[g], 0.0).astype(BF16) for g in range(GROUPS)]
    bcol = [bcol_ref[g] for g in range(GROUPS)]
    n_rg = rows // LANES
    mixed_g = [
        _dot(wm[g], jnp.concatenate(
            [vb[rg * LANES:(rg + 1) * LANES, g * LANES:(g + 1) * LANES] for rg in range(n_rg)],
            axis=1)) for g in range(GROUPS)]
    a_out = _dot(o_a, wa_ref[...])
    yield
    mixed = jnp.concatenate(
        [jnp.concatenate([mixed_g[g][:, rg * LANES:(rg + 1) * LANES] + bcol[g]
                          for g in range(GROUPS)], axis=1) for rg in range(n_rg)], axis=0)
    o_b = (u * mixed).astype(BF16)
    yield

    mix = (gate_a * a_out + gate_b * _dot(o_b, wb_ref[...])).astype(BF16)
    yield
    emit("h", x + _dot(mix, wo_ref[...]))


def _ffn_stream(h, p, wrefs, prev_rows, keep_rows, emit, *, rows, tpos):
    (nfw_ref, wup_ref, cw_ref, cb_ref, wdn_ref, npw_ref, wpg_ref, wpp_ref, fnw_ref) = wrefs
    d_ff = wdn_ref.shape[0]
    n_tiles = d_ff // FF_TILE
    hn = _rms(h, nfw_ref[...]).astype(BF16)

    def conv(up, lo):
        cols = slice(lo, lo + FF_TILE)
        w = cw_ref[:, cols]
        e0, e1 = prev_rows(cols)
        keep_rows(cols, up)
        x1 = jnp.where(tpos >= 1, pltpu.roll(up, 1, 0), e1)
        x2 = jnp.where(tpos >= 2, pltpu.roll(up, 2, 0), jnp.where(tpos == 1, e1, e0))
        return cb_ref[:, cols] + x2 * w[0:1, :] + x1 * w[1:2, :] + up * w[2:3, :]

    def up_pair(j):
        lo_a = j * FF_TILE
        lo_b = d_ff + j * FF_TILE
        return (_dot(hn, wup_ref[:, lo_a:lo_a + FF_TILE]), _dot(hn, wup_ref[:, lo_b:lo_b + FF_TILE]))

    acc = jnp.zeros((rows, h.shape[1]), F32)
    nxt = up_pair(0)
    pp = _dot(p.astype(BF16), wpp_ref[...])
    gated = None
    yield
    for j in range(n_tiles + 1):
        lo_a = j * FF_TILE
        cur = nxt
        if j + 1 < n_tiles:
            nxt = up_pair(j + 1)
        if j < n_tiles:
            ya = conv(cur[0], lo_a)
            yb = conv(cur[1], d_ff + lo_a)
            new_gated = (_gelu(ya) * yb).astype(BF16)
        if j > 0:
            acc = acc + _dot(gated, wdn_ref[lo_a - FF_TILE:lo_a, :])
        gated = new_gated
        yield
    h2 = h + acc
    gate = jax.nn.sigmoid(_dot(_rms(h2, npw_ref[...]).astype(BF16), wpg_ref[...]))
    yield
    h3 = h2 + gate * pp
    emit("y", _rms(h3, fnw_ref[...]))


def _prompt_kernel(*refs, rows, chunk, n_t, n_blocks):
    x_ref, p_ref = refs[:2]
    mixer_w = refs[2:13]
    ffn_w = refs[13:22]
    y_ref, sout_ref, cout_ref, st_s, carry_s, h_s = refs[22:]
    i = pl.program_id(0)
    t_mix = lax.rem(jnp.minimum(i, n_blocks - 1), n_t)
    t_ffn = lax.rem(jnp.maximum(i - 1, 0), n_t)

    @pl.when(i == 0)
    def _():
        h_s[...] = jnp.zeros_like(h_s)

    @pl.when(t_mix == 0)
    def _():
        st_s[...] = jnp.zeros_like(st_s)

    @pl.when(t_ffn == 0)
    def _():
        carry_s[...] = jnp.zeros_like(carry_s)

    slot = lax.rem(i, 2)
    results = {}

    def state_out(hd, states):
        st_s[hd] = states[-1]

    def prev_rows(cols):
        return carry_s[0:1, cols], carry_s[1:2, cols]

    def keep_rows(cols, up):
        carry_s[:, cols] = up[rows - 2:rows, :]
        cout_ref[0, :, cols] = up[rows - 2:rows, :]

    tpos = lax.broadcasted_iota(jnp.int32, (rows, FF_TILE), 0)
    _interleave([
        _mixer_stream(x_ref[...], mixer_w, lambda hd: st_s[hd], state_out, results.__setitem__,
                      rows=rows, chunk=chunk, scan=True),
        _ffn_stream(h_s[1 - slot], p_ref[...], ffn_w, prev_rows, keep_rows, results.__setitem__,
                    rows=rows, tpos=tpos),
    ], steps=(2, 1), lead=1)
    y_ref[...] = results["y"]
    h_s[slot] = results["h"]

    @pl.when((t_mix == n_t - 1) & (i < n_blocks))
    def _():
        for hd in range(HEADS):
            sout_ref[0, hd] = st_s[hd].T


def _mixer_sample_kernel(*refs, rows, seq):
    x_ref = refs[0]
    wrefs = refs[1:12]
    sin_ref, h_ref, sout_ref, vout_ref = refs[12:]
    n_seq = rows // seq
    results = {}

    def state_out(hd, states):
        for j in range(n_seq):
            sout_ref[j, hd] = states[j].T

    for _ in _mixer_stream(x_ref[...], wrefs,
                           lambda hd: [sin_ref[j, hd].T for j in range(n_seq)], state_out,
                           results.__setitem__, rows=rows, chunk=seq, scan=False):
        pass
    vout_ref[...] = results["v"]
    h_ref[...] = results["h"]


def _ffn_sample_kernel(*refs, rows, seq):
    h_ref, p_ref = refs[:2]
    wrefs = refs[2:11]
    cst_ref, y_ref, cout_ref = refs[11:]
    n_seq = rows // seq
    results = {}

    def prev_rows(cols):
        st = cst_ref[:, :, cols]
        width = st.shape[-1]
        return tuple(jnp.broadcast_to(st[:, r:r + 1, :], (n_seq, seq, width)).reshape(rows, width)
                     for r in range(CONV_W - 1))

    def keep_rows(cols, up):
        cout_ref[:, :, cols] = up.reshape(n_seq, seq, up.shape[-1])[:, seq - (CONV_W - 1):seq, :]

    tpos = lax.broadcasted_iota(jnp.int32, (rows, FF_TILE), 0) & (seq - 1)
    for _ in _ffn_stream(h_ref[...], p_ref[...], wrefs, prev_rows, keep_rows, results.__setitem__,
                         rows=rows, tpos=tpos):
        pass
    y_ref[...] = results["y"]


def _const_spec(shape):
    nd = len(shape)
    return pl.BlockSpec(shape, lambda *_: (0,) * nd, pipeline_mode=pl.Buffered(1))


def _params(n_axes):
    return pltpu.CompilerParams(dimension_semantics=("arbitrary",) * n_axes,
                                vmem_limit_bytes=VMEM_LIMIT)


def _prompt(x, p, mixer_consts, ffn_consts, *, rows, chunk):
    batch, seq, d = x.shape
    n_t = seq // rows
    n_blocks = batch * n_t
    width = ffn_consts[2].shape[1]
    mix_blk = lambda i: jnp.minimum(i, n_blocks - 1)
    ffn_blk = lambda i: jnp.maximum(i - 1, 0)
    consts = tuple(mixer_consts) + tuple(ffn_consts)
    in_specs = ([pl.BlockSpec((rows, d), lambda i: (mix_blk(i), 0)),
                 pl.BlockSpec((rows, p.shape[-1]), lambda i: (ffn_blk(i), 0))]
                + [_const_spec(a.shape) for a in consts])
    out_specs = [pl.BlockSpec((rows, d), lambda i: (ffn_blk(i), 0)),
                 pl.BlockSpec((1, HEADS, LANES, LANES), lambda i: (mix_blk(i) // n_t, 0, 0, 0)),
                 pl.BlockSpec((1, CONV_W - 1, width), lambda i: (ffn_blk(i) // n_t, 0, 0))]
    out_shape = [jax.ShapeDtypeStruct((batch * seq, d), F32),
                 jax.ShapeDtypeStruct((batch, HEADS, LANES, LANES), F32),
                 jax.ShapeDtypeStruct((batch, CONV_W - 1, width), F32)]
    scratch = [pltpu.VMEM((HEADS, LANES, LANES), F32), pltpu.VMEM((CONV_W - 1, width), F32),
               pltpu.VMEM((2, rows, d), F32)]
    y, st, cs = pl.pallas_call(
        functools.partial(_prompt_kernel, rows=rows, chunk=chunk, n_t=n_t, n_blocks=n_blocks),
        grid=(n_blocks + 1,), in_specs=in_specs, out_specs=out_specs, out_shape=out_shape,
        scratch_shapes=scratch, compiler_params=_params(1), name="prompt_step",
    )(x.reshape(batch * seq, d), p.reshape(batch * seq, -1), *consts)
    return y.reshape(batch, seq, d), st, cs


def _sample(x, p, state, conv_state, mixer_consts, ffn_consts, *, mixer_rows, ffn_rows):
    batch, seq, d = x.shape
    width = ffn_consts[2].shape[1]
    gw = GROUPS * LANES
    rows = mixer_rows
    n_seq = rows // seq
    row_map = lambda i: (i, 0)
    st_spec = pl.BlockSpec((n_seq, HEADS, LANES, LANES), lambda i: (i, 0, 0, 0))
    h, st, v = pl.pallas_call(
        functools.partial(_mixer_sample_kernel, rows=rows, seq=seq),
        grid=(batch // n_seq,),
        in_specs=([pl.BlockSpec((rows, d), row_map)] + [_const_spec(a.shape) for a in mixer_consts]
                  + [st_spec]),
        out_specs=[pl.BlockSpec((rows, d), row_map), st_spec, pl.BlockSpec((rows, gw), row_map)],
        out_shape=[jax.ShapeDtypeStruct((batch * seq, d), F32),
                   jax.ShapeDtypeStruct(state.shape, F32),
                   jax.ShapeDtypeStruct((batch * seq, gw), F32)],
        compiler_params=_params(1), name="mixer_sample",
    )(x.reshape(batch * seq, d), *mixer_consts, state)
    rows = ffn_rows
    n_seq = rows // seq
    cs_spec = pl.BlockSpec((n_seq, CONV_W - 1, width), lambda i: (i, 0, 0))
    y, cs = pl.pallas_call(
        functools.partial(_ffn_sample_kernel, rows=rows, seq=seq),
        grid=(batch // n_seq,),
        in_specs=([pl.BlockSpec((rows, d), row_map), pl.BlockSpec((rows, p.shape[-1]), row_map)]
                  + [_const_spec(a.shape) for a in ffn_consts] + [cs_spec]),
        out_specs=[pl.BlockSpec((rows, d), row_map), cs_spec],
        out_shape=[jax.ShapeDtypeStruct((batch * seq, d), F32),
                   jax.ShapeDtypeStruct(conv_state.shape, F32)],
        compiler_params=_params(1), name="ffn_sample",
    )(h, p.reshape(batch * seq, -1), *ffn_consts, conv_state)
    return y.reshape(batch, seq, d), st, cs, v.reshape(batch, seq, gw)


def kernel(x_prompt, x_sample, p_prompt, p_sample, state_hgrn, state_conv, lb_logits, norm_mix_w, w_in, hgrn_norm_w, ln_v_w, ln_v_b, w_spatial, b_spatial, w_a_out, w_b_out, w_o, norm_ffn_w, w_up, conv_w, conv_b, w_down, norm_ple_w, w_ple_gate, w_ple_proj, final_norm_w):
    depth = w_in.shape[0]
    assert depth == 1 and lb_logits.shape[0] == 2
    seq_s = x_sample.shape[1]
    assert seq_s == SUBLANES and LANES % seq_s == 0
    l = 0
    row = lambda a: a.reshape(1, -1)

    ws, bs = w_spatial[l], b_spatial[l]
    reps = LANES // seq_s
    eye = jnp.eye(reps, dtype=F32)
    wmix_s = jnp.stack([jnp.kron(eye, ws[g, :seq_s, :seq_s]) for g in range(GROUPS)])
    bcol_s = jnp.tile(bs[:, :seq_s], (1, reps))[:, :, None]

    def mixer_consts(wmix, bcol):
        return (lb_logits, row(norm_mix_w[l]), w_in[l].astype(BF16), row(hgrn_norm_w[l]),
                row(ln_v_w[l]), row(ln_v_b[l]), wmix, bcol, w_a_out[l].astype(BF16),
                w_b_out[l].astype(BF16), w_o[l].astype(BF16))

    ffn_consts = (row(norm_ffn_w[l]), w_up[l].astype(BF16), conv_w[l], row(conv_b[l]),
                  w_down[l].astype(BF16), row(norm_ple_w[l]), w_ple_gate[l].astype(BF16),
                  w_ple_proj[l].astype(BF16), row(final_norm_w))

    y_p, hs_p, cs_p = _prompt(x_prompt, p_prompt[l], mixer_consts(ws, bs[:, :, None]), ffn_consts,
                              rows=256, chunk=64)
    y_s, hs_s, cs_s, v_s = _sample(x_sample, p_sample[l], state_hgrn[l], state_conv[l],
                                   mixer_consts(wmix_s, bcol_s), ffn_consts,
                                   mixer_rows=LANES, ffn_rows=256)
    return (y_p, y_s, hs_p[None], hs_s[None], cs_p[None], cs_s[None], v_s[None])
```

```python
import functools

import jax
import jax.numpy as jnp
from jax import lax
from jax.experimental import pallas as pl
from jax.experimental.pallas import tpu as pltpu

F32 = jnp.float32
BF16 = jnp.bfloat16
EPS = 1e-6

HEADS = 4
GROUPS = 4
CONV_W = 3
LANES = 128
SUBLANES = 8
FF_TILE = 256
CAST_STEPS = 8
VMEM_LIMIT = 56 * 1024 * 1024

NT_DIMS = (((1,), (1,)), ((), ()))
TN_DIMS = (((0,), (0,)), ((), ()))


def _rms(x, w):
    return x * lax.rsqrt(jnp.mean(x * x, axis=-1, keepdims=True) + EPS) * w


def _gelu(x):
    return 0.5 * x * (1.0 + lax.erf(x * (2.0 ** -0.5)))


def _dot(a, b):
    return jnp.dot(a, b, preferred_element_type=F32)


def _hgrn_head(q, k, b, v, st_in, chunk, scan):
    rows = q.shape[0]
    n = rows // chunk
    shift = chunk.bit_length() - 1
    b3 = b.reshape(n, chunk, LANES)

    def chunk_row(r):
        return jnp.broadcast_to(b3[:, r:r + 1, :], b3.shape).reshape(rows, LANES)

    def blocked(a):
        cols = []
        for j in range(n):
            parts = [jnp.zeros((j * chunk, LANES), F32), a[j * chunk:(j + 1) * chunk],
                     jnp.zeros((rows - (j + 1) * chunk, LANES), F32)]
            cols.append(jnp.concatenate([p for p in parts if p.shape[0]], axis=0))
        return jnp.concatenate(cols, axis=1).astype(BF16)

    b_mid = chunk_row(chunk // 2 - 1)
    b_end = chunk_row(chunk - 1)
    q_t = (q * jnp.exp(b - b_mid)).astype(BF16)
    k_t = (k * jnp.exp(b_mid - b)).astype(BF16)
    vb = v.astype(BF16)
    k_blk = blocked(k * jnp.exp(b_end - b))
    q_blk = blocked(q * jnp.exp(b))
    yield
    scores = lax.dot_general(q_t, k_t, NT_DIMS, preferred_element_type=F32)
    ds = lax.dot_general(vb, k_blk, TN_DIMS, preferred_element_type=F32)
    yield
    row = lax.broadcasted_iota(jnp.int32, (rows, rows), 0)
    col = lax.broadcasted_iota(jnp.int32, (rows, rows), 1)
    keep = ((row >> shift) == (col >> shift)) & (col <= row)
    scores = jnp.where(keep, scores, 0.0).astype(BF16)
    before, after = [], []
    st = st_in
    for j in range(n):
        st_j = st if scan else st_in[j]
        before.append(st_j.astype(BF16))
        st = st_j * jnp.exp(b3[j, chunk - 1:chunk, :]) + ds[:, j * LANES:(j + 1) * LANES]
        after.append(st)
    s_rhs = jnp.concatenate(before, axis=1)
    yield
    o = _dot(scores, vb) + lax.dot_general(q_blk, s_rhs, NT_DIMS, preferred_element_type=F32)
    return o, after


def _interleave(streams, steps, lead=0):
    done = [False] * len(streams)
    for _ in range(lead):
        next(streams[0])
    while not all(done):
        for s, gen in enumerate(streams):
            for _ in range(steps[s]):
                if not done[s]:
                    try:
                        next(gen)
                    except StopIteration:
                        done[s] = True


def _mixer_stream(x, wrefs, state_in, state_out, emit, *, rows, chunk, scan):
    (lbl_ref, nmw_ref, win_ref, hnw_ref, lvw_ref, lvb_ref, wmix_ref, bcol_ref,
     wa_ref, wb_ref, wo_ref) = wrefs
    fdim = HEADS * LANES
    gw = GROUPS * LANES
    d_model = x.shape[1]
    offs = [0]
    for width in (fdim, fdim, fdim, fdim, gw, gw, d_model, d_model):
        offs.append(offs[-1] + width)

    xn = _rms(x, nmw_ref[...]).astype(BF16)

    def proj(i):
        return _dot(xn, win_ref[:, offs[i]:offs[i + 1]])

    lbl = lbl_ref[...]
    lbe = jnp.exp(lbl - jnp.max(lbl, axis=0, keepdims=True))
    lb = lbe[0:1, :] / jnp.sum(lbe, axis=0, keepdims=True)

    fg = lb + (1.0 - lb) * jax.nn.sigmoid(proj(1))
    k = 1.0 - fg
    log_f = jnp.log(fg)
    lf_hi = log_f.astype(BF16)
    lf_rem = log_f - lf_hi.astype(F32)
    lf_mid = lf_rem.astype(BF16)
    lf_lo = (lf_rem - lf_mid.astype(F32)).astype(BF16)
    lf3 = jnp.concatenate([lf_hi, lf_mid, lf_lo], axis=1)
    yield
    q = jax.nn.silu(proj(0))
    yield
    v_in = proj(2)
    yield
    r128 = lax.broadcasted_iota(jnp.int32, (LANES, LANES), 0)
    c128 = lax.broadcasted_iota(jnp.int32, (LANES, LANES), 1)
    shift = min(chunk, LANES).bit_length() - 1
    seg = jnp.where(((r128 >> shift) == (c128 >> shift)) & (c128 <= r128), 1.0, 0.0).astype(BF16)
    b_rows = []
    for g in range(rows // LANES):
        b3 = _dot(seg, lf3[g * LANES:(g + 1) * LANES, :])
        b_rows.append(b3[:, :fdim] + b3[:, fdim:2 * fdim] + b3[:, 2 * fdim:])
    b = jnp.concatenate(b_rows, axis=0)
    yield

    def head(hd):
        cols = slice(hd * LANES, (hd + 1) * LANES)
        o, st_out = yield from _hgrn_head(q[:, cols], k[:, cols], b[:, cols], v_in[:, cols],
                                          state_in(hd), chunk, scan)
        state_out(hd, st_out)
        return o

    o_heads = [(yield from head(0))]
    yield
    og = jax.nn.sigmoid(proj(3))
    yield
    o_heads.append((yield from head(1)))
    yield
    u = _gelu(proj(4))
    yield
    o_heads.append((yield from head(2)))
    yield
    gv = _gelu(proj(5))
    gc = gv - jnp.mean(gv, axis=-1, keepdims=True)
    v = gc * lax.rsqrt(jnp.mean(gc * gc, axis=-1, keepdims=True) + EPS) * lvw_ref[...] + lvb_ref[...]
    emit("v", v)
    yield
    o_heads.append((yield from head(3)))
    yield
    gate_a = jax.nn.sigmoid(proj(6))
    yield
    gate_b = jax.nn.sigmoid(proj(7))
    yield
    o_a = _rms(jnp.concatenate(o_heads, axis=1) * og, hnw_ref[...]).astype(BF16)

    vb = v.astype(BF16)
    wm = [jnp.where(c128 <= r128, wmix_ref[g], 0.0).astype(BF16) for g in range(GROUPS)]
    bcol = [bcol_ref[g] for g in range(GROUPS)]
    n_rg = rows // LANES
    mixed_g = [
        _dot(wm[g], jnp.concatenate(
            [vb[rg * LANES:(rg + 1) * LANES, g * LANES:(g + 1) * LANES] for rg in range(n_rg)],
            axis=1)) for g in range(GROUPS)]
    a_out = _dot(o_a, wa_ref[...])
    yield
    mixed = jnp.concatenate(
        [jnp.concatenate([mixed_g[g][:, rg * LANES:(rg + 1) * LANES] + bcol[g]
                          for g in range(GROUPS)], axis=1) for rg in range(n_rg)], axis=0)
    o_b = (u * mixed).astype(BF16)
    yield

    mix = (gate_a * a_out + gate_b * _dot(o_b, wb_ref[...])).astype(BF16)
    yield
    emit("h", x + _dot(mix, wo_ref[...]))


def _ffn_stream(h, p, wrefs, prev_rows, keep_rows, emit, *, rows, tpos):
    (nfw_ref, wup_ref, cw_ref, cb_ref, wdn_ref, npw_ref, wpg_ref, wpp_ref, fnw_ref) = wrefs
    d_ff = wdn_ref.shape[0]
    n_tiles = d_ff // FF_TILE
    hn = _rms(h, nfw_ref[...]).astype(BF16)

    def conv(up, lo):
        cols = slice(lo, lo + FF_TILE)
        w = cw_ref[:, cols]
        e0, e1 = prev_rows(cols)
        keep_rows(cols, up)
        x1 = jnp.where(tpos >= 1, pltpu.roll(up, 1, 0), e1)
        x2 = jnp.where(tpos >= 2, pltpu.roll(up, 2, 0), jnp.where(tpos == 1, e1, e0))
        return cb_ref[:, cols] + x2 * w[0:1, :] + x1 * w[1:2, :] + up * w[2:3, :]

    def up_pair(j):
        lo_a = j * FF_TILE
        lo_b = d_ff + j * FF_TILE
        return (_dot(hn, wup_ref[:, lo_a:lo_a + FF_TILE]), _dot(hn, wup_ref[:, lo_b:lo_b + FF_TILE]))

    acc = jnp.zeros((rows, h.shape[1]), F32)
    nxt = up_pair(0)
    pp = _dot(p.astype(BF16), wpp_ref[...])
    gated = None
    yield
    for j in range(n_tiles + 1):
        lo_a = j * FF_TILE
        cur = nxt
        if j + 1 < n_tiles:
            nxt = up_pair(j + 1)
        if j < n_tiles:
            ya = conv(cur[0], lo_a)
            yb = conv(cur[1], d_ff + lo_a)
            new_gated = (_gelu(ya) * yb).astype(BF16)
        if j > 0:
            acc = acc + _dot(gated, wdn_ref[lo_a - FF_TILE:lo_a, :])
        gated = new_gated
        yield
    h2 = h + acc
    gate = jax.nn.sigmoid(_dot(_rms(h2, npw_ref[...]).astype(BF16), wpg_ref[...]))
    yield
    h3 = h2 + gate * pp
    emit("y", _rms(h3, fnw_ref[...]))


def _prompt_kernel(*refs, rows, chunk, n_t, n_blocks):
    x_ref, p_ref = refs[:2]
    mixer_w = refs[2:13]
    ffn_w = refs[13:22]
    y_ref, sout_ref, cout_ref, st_s, carry_s, h_s = refs[22:]
    i = pl.program_id(0)
    t_mix = lax.rem(jnp.minimum(i, n_blocks - 1), n_t)
    t_ffn = lax.rem(jnp.maximum(i - 1, 0), n_t)

    @pl.when(i == 0)
    def _():
        h_s[...] = jnp.zeros_like(h_s)

    @pl.when(t_mix == 0)
    def _():
        st_s[...] = jnp.zeros_like(st_s)

    @pl.when(t_ffn == 0)
    def _():
        carry_s[...] = jnp.zeros_like(carry_s)

    slot = lax.rem(i, 2)
    results = {}

    def state_out(hd, states):
        st_s[hd] = states[-1]

    def prev_rows(cols):
        return carry_s[0:1, cols], carry_s[1:2, cols]

    def keep_rows(cols, up):
        carry_s[:, cols] = up[rows - 2:rows, :]
        cout_ref[0, :, cols] = up[rows - 2:rows, :]

    tpos = lax.broadcasted_iota(jnp.int32, (rows, FF_TILE), 0)
    _interleave([
        _mixer_stream(x_ref[...], mixer_w, lambda hd: st_s[hd], state_out, results.__setitem__,
                      rows=rows, chunk=chunk, scan=True),
        _ffn_stream(h_s[1 - slot], p_ref[...], ffn_w, prev_rows, keep_rows, results.__setitem__,
                    rows=rows, tpos=tpos),
    ], steps=(2, 1), lead=1)
    y_ref[...] = results["y"]
    h_s[slot] = results["h"]

    @pl.when((t_mix == n_t - 1) & (i < n_blocks))
    def _():
        for hd in range(HEADS):
            sout_ref[0, hd] = st_s[hd].T


def _mixer_sample_kernel(*refs, rows, seq):
    x_ref = refs[0]
    wrefs = refs[1:12]
    sin_ref, h_ref, sout_ref, vout_ref = refs[12:]
    n_seq = rows // seq
    results = {}

    def state_out(hd, states):
        for j in range(n_seq):
            sout_ref[j, hd] = states[j].T

    for _ in _mixer_stream(x_ref[...], wrefs,
                           lambda hd: [sin_ref[j, hd].T for j in range(n_seq)], state_out,
                           results.__setitem__, rows=rows, chunk=seq, scan=False):
        pass
    vout_ref[...] = results["v"]
    h_ref[...] = results["h"]


def _ffn_sample_kernel(*refs, rows, seq):
    h_ref, p_ref = refs[:2]
    wrefs = refs[2:11]
    cst_ref, y_ref, cout_ref = refs[11:]
    n_seq = rows // seq
    results = {}

    def prev_rows(cols):
        st = cst_ref[:, :, cols]
        width = st.shape[-1]
        return tuple(jnp.broadcast_to(st[:, r:r + 1, :], (n_seq, seq, width)).reshape(rows, width)
                     for r in range(CONV_W - 1))

    def keep_rows(cols, up):
        cout_ref[:, :, cols] = up.reshape(n_seq, seq, up.shape[-1])[:, seq - (CONV_W - 1):seq, :]

    tpos = lax.broadcasted_iota(jnp.int32, (rows, FF_TILE), 0) & (seq - 1)
    for _ in _ffn_stream(h_ref[...], p_ref[...], wrefs, prev_rows, keep_rows, results.__setitem__,
                         rows=rows, tpos=tpos):
        pass
    y_ref[...] = results["y"]


def _const_spec(shape):
    nd = len(shape)
    return pl.BlockSpec(shape, lambda *_: (0,) * nd, pipeline_mode=pl.Buffered(1))


def _params(n_axes):
    return pltpu.CompilerParams(dimension_semantics=("arbitrary",) * n_axes,
                                vmem_limit_bytes=VMEM_LIMIT)


def _cast_kernel(*refs):
    n = len(refs) // 2
    for src, dst in zip(refs[:n], refs[n:]):
        dst[...] = src[...].astype(dst.dtype)


def _to_bf16(mats):
    tile_rows = 2 * SUBLANES
    for m in mats:
        assert m.shape[0] % (CAST_STEPS * tile_rows) == 0 and m.shape[1] % LANES == 0
    specs = [pl.BlockSpec((m.shape[0] // CAST_STEPS, m.shape[1]), lambda i: (i, 0)) for m in mats]
    return pl.pallas_call(
        _cast_kernel, grid=(CAST_STEPS,), in_specs=specs, out_specs=specs,
        out_shape=[jax.ShapeDtypeStruct(m.shape, BF16) for m in mats],
        compiler_params=_params(1), name="weights_to_bf16",
    )(*mats)


def _prompt(x, p, mixer_consts, ffn_consts, *, rows, chunk):
    batch, seq, d = x.shape
    n_t = seq // rows
    n_blocks = batch * n_t
    width = ffn_consts[2].shape[1]
    mix_blk = lambda i: jnp.minimum(i, n_blocks - 1)
    ffn_blk = lambda i: jnp.maximum(i - 1, 0)
    consts = tuple(mixer_consts) + tuple(ffn_consts)
    in_specs = ([pl.BlockSpec((rows, d), lambda i: (mix_blk(i), 0)),
                 pl.BlockSpec((rows, p.shape[-1]), lambda i: (ffn_blk(i), 0))]
                + [_const_spec(a.shape) for a in consts])
    out_specs = [pl.BlockSpec((rows, d), lambda i: (ffn_blk(i), 0)),
                 pl.BlockSpec((1, HEADS, LANES, LANES), lambda i: (mix_blk(i) // n_t, 0, 0, 0)),
                 pl.BlockSpec((1, CONV_W - 1, width), lambda i: (ffn_blk(i) // n_t, 0, 0))]
    out_shape = [jax.ShapeDtypeStruct((batch * seq, d), F32),
                 jax.ShapeDtypeStruct((batch, HEADS, LANES, LANES), F32),
                 jax.ShapeDtypeStruct((batch, CONV_W - 1, width), F32)]
    scratch = [pltpu.VMEM((HEADS, LANES, LANES), F32), pltpu.VMEM((CONV_W - 1, width), F32),
               pltpu.VMEM((2, rows, d), F32)]
    y, st, cs = pl.pallas_call(
        functools.partial(_prompt_kernel, rows=rows, chunk=chunk, n_t=n_t, n_blocks=n_blocks),
        grid=(n_blocks + 1,), in_specs=in_specs, out_specs=out_specs, out_shape=out_shape,
        scratch_shapes=scratch, compiler_params=_params(1), name="prompt_step",
    )(x.reshape(batch * seq, d), p.reshape(batch * seq, -1), *consts)
    return y.reshape(batch, seq, d), st, cs


def _sample(x, p, state, conv_state, mixer_consts, ffn_consts, *, mixer_rows, ffn_rows):
    batch, seq, d = x.shape
    width = ffn_consts[2].shape[1]
    gw = GROUPS * LANES
    rows = mixer_rows
    n_seq = rows // seq
    row_map = lambda i: (i, 0)
    st_spec = pl.BlockSpec((n_seq, HEADS, LANES, LANES), lambda i: (i, 0, 0, 0))
    h, st, v = pl.pallas_call(
        functools.partial(_mixer_sample_kernel, rows=rows, seq=seq),
        grid=(batch // n_seq,),
        in_specs=([pl.BlockSpec((rows, d), row_map)] + [_const_spec(a.shape) for a in mixer_consts]
                  + [st_spec]),
        out_specs=[pl.BlockSpec((rows, d), row_map), st_spec, pl.BlockSpec((rows, gw), row_map)],
        out_shape=[jax.ShapeDtypeStruct((batch * seq, d), F32),
                   jax.ShapeDtypeStruct(state.shape, F32),
                   jax.ShapeDtypeStruct((batch * seq, gw), F32)],
        compiler_params=_params(1), name="mixer_sample",
    )(x.reshape(batch * seq, d), *mixer_consts, state)
    rows = ffn_rows
    n_seq = rows // seq
    cs_spec = pl.BlockSpec((n_seq, CONV_W - 1, width), lambda i: (i, 0, 0))
    y, cs = pl.pallas_call(
        functools.partial(_ffn_sample_kernel, rows=rows, seq=seq),
        grid=(batch // n_seq,),
        in_specs=([pl.BlockSpec((rows, d), row_map), pl.BlockSpec((rows, p.shape[-1]), row_map)]
                  + [_const_spec(a.shape) for a in ffn_consts] + [cs_spec]),
        out_specs=[pl.BlockSpec((rows, d), row_map), cs_spec],
        out_shape=[jax.ShapeDtypeStruct((batch * seq, d), F32),
                   jax.ShapeDtypeStruct(conv_state.shape, F32)],
        compiler_params=_params(1), name="ffn_sample",
    )(h, p.reshape(batch * seq, -1), *ffn_consts, conv_state)
    return y.reshape(batch, seq, d), st, cs, v.reshape(batch, seq, gw)


def kernel(x_prompt, x_sample, p_prompt, p_sample, state_hgrn, state_conv, lb_logits, norm_mix_w, w_in, hgrn_norm_w, ln_v_w, ln_v_b, w_spatial, b_spatial, w_a_out, w_b_out, w_o, norm_ffn_w, w_up, conv_w, conv_b, w_down, norm_ple_w, w_ple_gate, w_ple_proj, final_norm_w):
    depth = w_in.shape[0]
    assert depth == 1 and lb_logits.shape[0] == 2
    seq_s = x_sample.shape[1]
    assert seq_s == SUBLANES and LANES % seq_s == 0
    l = 0
    row = lambda a: a.reshape(1, -1)

    ws, bs = w_spatial[l], b_spatial[l]
    reps = LANES // seq_s
    eye = jnp.eye(reps, dtype=F32)
    wmix_s = jnp.stack([jnp.kron(eye, ws[g, :seq_s, :seq_s]) for g in range(GROUPS)])
    bcol_s = jnp.tile(bs[:, :seq_s], (1, reps))[:, :, None]

    (w_in_b, w_a_b, w_b_b, w_o_b, w_up_b, w_down_b, w_pg_b, w_pp_b) = _to_bf16(
        [w_in[l], w_a_out[l], w_b_out[l], w_o[l], w_up[l], w_down[l], w_ple_gate[l], w_ple_proj[l]])

    def mixer_consts(wmix, bcol):
        return (lb_logits, row(norm_mix_w[l]), w_in_b, row(hgrn_norm_w[l]),
                row(ln_v_w[l]), row(ln_v_b[l]), wmix, bcol, w_a_b, w_b_b, w_o_b)

    ffn_consts = (row(norm_ffn_w[l]), w_up_b, conv_w[l], row(conv_b[l]), w_down_b,
                  row(norm_ple_w[l]), w_pg_b, w_pp_b, row(final_norm_w))

    y_p, hs_p, cs_p = _prompt(x_prompt, p_prompt[l], mixer_consts(ws, bs[:, :, None]), ffn_consts,
                              rows=256, chunk=64)
    y_s, hs_s, cs_s, v_s = _sample(x_sample, p_sample[l], state_hgrn[l], state_conv[l],
                                   mixer_consts(wmix_s, bcol_s), ffn_consts,
                                   mixer_rows=LANES, ffn_rows=256)
    return (y_p, y_s, hs_p[None], hs_s[None], cs_p[None], cs_s[None], v_s[None])
```

```python
import functools

import jax
import jax.numpy as jnp
from jax import lax
from jax.experimental import pallas as pl
from jax.experimental.pallas import tpu as pltpu

F32 = jnp.float32
BF16 = jnp.bfloat16
EPS = 1e-6

HEADS = 4
GROUPS = 4
CONV_W = 3
LANES = 128
SUBLANES = 8
FF_TILE = 256
CAST_STEPS = 8
VMEM_LIMIT = 56 * 1024 * 1024

NT_DIMS = (((1,), (1,)), ((), ()))
TN_DIMS = (((0,), (0,)), ((), ()))


def _rms(x, w):
    return x * lax.rsqrt(jnp.mean(x * x, axis=-1, keepdims=True) + EPS) * w


def _gelu(x):
    return 0.5 * x * (1.0 + lax.erf(x * (2.0 ** -0.5)))


def _dot(a, b):
    return jnp.dot(a, b, preferred_element_type=F32)


def _hgrn_head(q, k, b, v, st_in, chunk, scan):
    rows = q.shape[0]
    n = rows // chunk
    shift = chunk.bit_length() - 1
    b3 = b.reshape(n, chunk, LANES)

    def chunk_row(r):
        return jnp.broadcast_to(b3[:, r:r + 1, :], b3.shape).reshape(rows, LANES)

    def blocked(a):
        cols = []
        for j in range(n):
            parts = [jnp.zeros((j * chunk, LANES), F32), a[j * chunk:(j + 1) * chunk],
                     jnp.zeros((rows - (j + 1) * chunk, LANES), F32)]
            cols.append(jnp.concatenate([p for p in parts if p.shape[0]], axis=0))
        return jnp.concatenate(cols, axis=1).astype(BF16)

    b_mid = chunk_row(chunk // 2 - 1)
    b_end = chunk_row(chunk - 1)
    q_t = (q * jnp.exp(b - b_mid)).astype(BF16)
    k_t = (k * jnp.exp(b_mid - b)).astype(BF16)
    vb = v.astype(BF16)
    k_blk = blocked(k * jnp.exp(b_end - b))
    q_blk = blocked(q * jnp.exp(b))
    yield
    scores = lax.dot_general(q_t, k_t, NT_DIMS, preferred_element_type=F32)
    ds = lax.dot_general(vb, k_blk, TN_DIMS, preferred_element_type=F32)
    yield
    row = lax.broadcasted_iota(jnp.int32, (rows, rows), 0)
    col = lax.broadcasted_iota(jnp.int32, (rows, rows), 1)
    keep = ((row >> shift) == (col >> shift)) & (col <= row)
    scores = jnp.where(keep, scores, 0.0).astype(BF16)
    before, after = [], []
    st = st_in
    for j in range(n):
        st_j = st if scan else st_in[j]
        before.append(st_j.astype(BF16))
        st = st_j * jnp.exp(b3[j, chunk - 1:chunk, :]) + ds[:, j * LANES:(j + 1) * LANES]
        after.append(st)
    s_rhs = jnp.concatenate(before, axis=1)
    yield
    o = _dot(scores, vb) + lax.dot_general(q_blk, s_rhs, NT_DIMS, preferred_element_type=F32)
    return o, after


def _interleave(streams, steps, lead=0):
    done = [False] * len(streams)
    for _ in range(lead):
        next(streams[0])
    while not all(done):
        for s, gen in enumerate(streams):
            for _ in range(steps[s]):
                if not done[s]:
                    try:
                        next(gen)
                    except StopIteration:
                        done[s] = True


def _mixer_stream(x, wrefs, state_in, state_out, emit, *, rows, chunk, scan):
    (lbl_ref, nmw_ref, win_ref, hnw_ref, lvw_ref, lvb_ref, wmix_ref, bcol_ref,
     wa_ref, wb_ref, wo_ref) = wrefs
    fdim = HEADS * LANES
    gw = GROUPS * LANES
    d_model = x.shape[1]
    offs = [0]
    for width in (fdim, fdim, fdim, fdim, gw, gw, d_model, d_model):
        offs.append(offs[-1] + width)

    xn = _rms(x, nmw_ref[...]).astype(BF16)

    def proj(i):
        return _dot(xn, win_ref[:, offs[i]:offs[i + 1]])

    lbl = lbl_ref[...]
    lbe = jnp.exp(lbl - jnp.max(lbl, axis=0, keepdims=True))
    lb = lbe[0:1, :] / jnp.sum(lbe, axis=0, keepdims=True)

    fg = lb + (1.0 - lb) * jax.nn.sigmoid(proj(1))
    k = 1.0 - fg
    log_f = jnp.log(fg)
    lf_hi = log_f.astype(BF16)
    lf_rem = log_f - lf_hi.astype(F32)
    lf_mid = lf_rem.astype(BF16)
    lf_lo = (lf_rem - lf_mid.astype(F32)).astype(BF16)
    lf3 = jnp.concatenate([lf_hi, lf_mid, lf_lo], axis=1)
    yield
    q = jax.nn.silu(proj(0))
    yield
    v_in = proj(2)
    yield
    r128 = lax.broadcasted_iota(jnp.int32, (LANES, LANES), 0)
    c128 = lax.broadcasted_iota(jnp.int32, (LANES, LANES), 1)
    shift = min(chunk, LANES).bit_length() - 1
    seg = jnp.where(((r128 >> shift) == (c128 >> shift)) & (c128 <= r128), 1.0, 0.0).astype(BF16)
    b_rows = []
    for g in range(rows // LANES):
        b3 = _dot(seg, lf3[g * LANES:(g + 1) * LANES, :])
        b_rows.append(b3[:, :fdim] + b3[:, fdim:2 * fdim] + b3[:, 2 * fdim:])
    b = jnp.concatenate(b_rows, axis=0)
    yield

    def head(hd):
        cols = slice(hd * LANES, (hd + 1) * LANES)
        o, st_out = yield from _hgrn_head(q[:, cols], k[:, cols], b[:, cols], v_in[:, cols],
                                          state_in(hd), chunk, scan)
        state_out(hd, st_out)
        return o

    o_heads = [(yield from head(0))]
    yield
    og = jax.nn.sigmoid(proj(3))
    yield
    o_heads.append((yield from head(1)))
    yield
    u = _gelu(proj(4))
    yield
    o_heads.append((yield from head(2)))
    yield
    gv = _gelu(proj(5))
    gc = gv - jnp.mean(gv, axis=-1, keepdims=True)
    v = gc * lax.rsqrt(jnp.mean(gc * gc, axis=-1, keepdims=True) + EPS) * lvw_ref[...] + lvb_ref[...]
    emit("v", v)
    yield
    o_heads.append((yield from head(3)))
    yield
    gate_a = jax.nn.sigmoid(proj(6))
    yield
    gate_b = jax.nn.sigmoid(proj(7))
    yield
    o_a = _rms(jnp.concatenate(o_heads, axis=1) * og, hnw_ref[...]).astype(BF16)

    vb = v.astype(BF16)
    wm = [jnp.where(c128 <= r128, wmix_ref[g], 0.0).astype(BF16) for g in range(GROUPS)]
    bcol = [bcol_ref[g] for g in range(GROUPS)]
    n_rg = rows // LANES
    mixed_g = [
        _dot(wm[g], jnp.concatenate(
            [vb[rg * LANES:(rg + 1) * LANES, g * LANES:(g + 1) * LANES] for rg in range(n_rg)],
            axis=1)) for g in range(GROUPS)]
    a_out = _dot(o_a, wa_ref[...])
    yield
    mixed = jnp.concatenate(
        [jnp.concatenate([mixed_g[g][:, rg * LANES:(rg + 1) * LANES] + bcol[g]
                          for g in range(GROUPS)], axis=1) for rg in range(n_rg)], axis=0)
    o_b = (u * mixed).astype(BF16)
    yield

    mix = (gate_a * a_out + gate_b * _dot(o_b, wb_ref[...])).astype(BF16)
    yield
    emit("h", x + _dot(mix, wo_ref[...]))


def _ffn_stream(h, p, wrefs, prev_rows, keep_rows, emit, *, rows, tpos):
    (nfw_ref, wup_ref, cw_ref, cb_ref, wdn_ref, npw_ref, wpg_ref, wpp_ref, fnw_ref) = wrefs
    d_ff = wdn_ref.shape[0]
    n_tiles = d_ff // FF_TILE
    hn = _rms(h, nfw_ref[...]).astype(BF16)

    def conv(up, lo):
        cols = slice(lo, lo + FF_TILE)
        w = cw_ref[:, cols]
        e0, e1 = prev_rows(cols)
        keep_rows(cols, up)
        x1 = jnp.where(tpos >= 1, pltpu.roll(up, 1, 0), e1)
        x2 = jnp.where(tpos >= 2, pltpu.roll(up, 2, 0), jnp.where(tpos == 1, e1, e0))
        return cb_ref[:, cols] + x2 * w[0:1, :] + x1 * w[1:2, :] + up * w[2:3, :]

    def up_pair(j):
        lo_a = j * FF_TILE
        lo_b = d_ff + j * FF_TILE
        return (_dot(hn, wup_ref[:, lo_a:lo_a + FF_TILE]), _dot(hn, wup_ref[:, lo_b:lo_b + FF_TILE]))

    acc = jnp.zeros((rows, h.shape[1]), F32)
    nxt = up_pair(0)
    pp = _dot(p.astype(BF16), wpp_ref[...])
    gated = None
    yield
    for j in range(n_tiles + 1):
        lo_a = j * FF_TILE
        cur = nxt
        if j + 1 < n_tiles:
            nxt = up_pair(j + 1)
        if j < n_tiles:
            ya = conv(cur[0], lo_a)
            yb = conv(cur[1], d_ff + lo_a)
            new_gated = (_gelu(ya) * yb).astype(BF16)
        if j > 0:
            acc = acc + _dot(gated, wdn_ref[lo_a - FF_TILE:lo_a, :])
        gated = new_gated
        yield
    h2 = h + acc
    gate = jax.nn.sigmoid(_dot(_rms(h2, npw_ref[...]).astype(BF16), wpg_ref[...]))
    yield
    h3 = h2 + gate * pp
    emit("y", _rms(h3, fnw_ref[...]))


def _prompt_kernel(*refs, rows, chunk, n_t, n_blocks):
    x_ref, p_ref = refs[:2]
    mixer_w = refs[2:13]
    ffn_w = refs[13:22]
    y_ref, sout_ref, cout_ref, st_s, carry_s, h_s = refs[22:]
    i = pl.program_id(0)
    t_mix = lax.rem(i, n_t)
    t_ffn = lax.rem(jnp.maximum(i - 1, 0), n_t)

    @pl.when(t_mix == 0)
    def _():
        st_s[...] = jnp.zeros_like(st_s)

    @pl.when(t_ffn == 0)
    def _():
        carry_s[...] = jnp.zeros_like(carry_s)

    slot = lax.rem(i, 2)

    def state_out(hd, states):
        st_s[hd] = states[-1]

    def prev_rows(cols):
        return carry_s[0:1, cols], carry_s[1:2, cols]

    def keep_rows(cols, up):
        carry_s[:, cols] = up[rows - 2:rows, :]
        cout_ref[0, :, cols] = up[rows - 2:rows, :]

    tpos = lax.broadcasted_iota(jnp.int32, (rows, FF_TILE), 0)

    def step(mixer, ffn):
        results = {}
        streams = []
        if mixer:
            streams.append(_mixer_stream(x_ref[...], mixer_w, lambda hd: st_s[hd], state_out,
                                         results.__setitem__, rows=rows, chunk=chunk, scan=True))
        if ffn:
            streams.append(_ffn_stream(h_s[1 - slot], p_ref[...], ffn_w, prev_rows, keep_rows,
                                       results.__setitem__, rows=rows, tpos=tpos))
        _interleave(streams, steps=(2, 1) if mixer and ffn else (1,), lead=1)
        if ffn:
            y_ref[...] = results["y"]
        if mixer:
            h_s[slot] = results["h"]

    pl.when(i == 0)(functools.partial(step, True, False))
    pl.when((i > 0) & (i < n_blocks))(functools.partial(step, True, True))
    pl.when(i == n_blocks)(functools.partial(step, False, True))

    @pl.when((t_mix == n_t - 1) & (i < n_blocks))
    def _():
        for hd in range(HEADS):
            sout_ref[0, hd] = st_s[hd].T


def _mixer_sample_kernel(*refs, rows, seq):
    x_ref = refs[0]
    wrefs = refs[1:12]
    sin_ref, h_ref, sout_ref, vout_ref = refs[12:]
    n_seq = rows // seq
    results = {}

    def state_out(hd, states):
        for j in range(n_seq):
            sout_ref[j, hd] = states[j].T

    for _ in _mixer_stream(x_ref[...], wrefs,
                           lambda hd: [sin_ref[j, hd].T for j in range(n_seq)], state_out,
                           results.__setitem__, rows=rows, chunk=seq, scan=False):
        pass
    vout_ref[...] = results["v"]
    h_ref[...] = results["h"]


def _ffn_sample_kernel(*refs, rows, seq):
    h_ref, p_ref = refs[:2]
    wrefs = refs[2:11]
    cst_ref, y_ref, cout_ref = refs[11:]
    n_seq = rows // seq
    results = {}

    def prev_rows(cols):
        st = cst_ref[:, :, cols]
        width = st.shape[-1]
        return tuple(jnp.broadcast_to(st[:, r:r + 1, :], (n_seq, seq, width)).reshape(rows, width)
                     for r in range(CONV_W - 1))

    def keep_rows(cols, up):
        cout_ref[:, :, cols] = up.reshape(n_seq, seq, up.shape[-1])[:, seq - (CONV_W - 1):seq, :]

    tpos = lax.broadcasted_iota(jnp.int32, (rows, FF_TILE), 0) & (seq - 1)
    for _ in _ffn_stream(h_ref[...], p_ref[...], wrefs, prev_rows, keep_rows, results.__setitem__,
                         rows=rows, tpos=tpos):
        pass
    y_ref[...] = results["y"]


def _const_spec(shape):
    nd = len(shape)
    return pl.BlockSpec(shape, lambda *_: (0,) * nd, pipeline_mode=pl.Buffered(1))


def _params(n_axes):
    return pltpu.CompilerParams(dimension_semantics=("arbitrary",) * n_axes,
                                vmem_limit_bytes=VMEM_LIMIT)


def _cast_kernel(*refs):
    n = len(refs) // 2
    for src, dst in zip(refs[:n], refs[n:]):
        dst[...] = src[...].astype(dst.dtype)


def _to_bf16(mats):
    tile_rows = 2 * SUBLANES
    for m in mats:
        assert m.shape[0] % (CAST_STEPS * tile_rows) == 0 and m.shape[1] % LANES == 0
    specs = [pl.BlockSpec((m.shape[0] // CAST_STEPS, m.shape[1]), lambda i: (i, 0)) for m in mats]
    return pl.pallas_call(
        _cast_kernel, grid=(CAST_STEPS,), in_specs=specs, out_specs=specs,
        out_shape=[jax.ShapeDtypeStruct(m.shape, BF16) for m in mats],
        compiler_params=_params(1), name="weights_to_bf16",
    )(*mats)


def _prompt(x, p, mixer_consts, ffn_consts, *, rows, chunk):
    batch, seq, d = x.shape
    n_t = seq // rows
    n_blocks = batch * n_t
    width = ffn_consts[2].shape[1]
    mix_blk = lambda i: jnp.minimum(i, n_blocks - 1)
    ffn_blk = lambda i: jnp.maximum(i - 1, 0)
    consts = tuple(mixer_consts) + tuple(ffn_consts)
    in_specs = ([pl.BlockSpec((rows, d), lambda i: (mix_blk(i), 0)),
                 pl.BlockSpec((rows, p.shape[-1]), lambda i: (ffn_blk(i), 0))]
                + [_const_spec(a.shape) for a in consts])
    out_specs = [pl.BlockSpec((rows, d), lambda i: (ffn_blk(i), 0)),
                 pl.BlockSpec((1, HEADS, LANES, LANES), lambda i: (mix_blk(i) // n_t, 0, 0, 0)),
                 pl.BlockSpec((1, CONV_W - 1, width), lambda i: (ffn_blk(i) // n_t, 0, 0))]
    out_shape = [jax.ShapeDtypeStruct((batch * seq, d), F32),
                 jax.ShapeDtypeStruct((batch, HEADS, LANES, LANES), F32),
                 jax.ShapeDtypeStruct((batch, CONV_W - 1, width), F32)]
    scratch = [pltpu.VMEM((HEADS, LANES, LANES), F32), pltpu.VMEM((CONV_W - 1, width), F32),
               pltpu.VMEM((2, rows, d), F32)]
    y, st, cs = pl.pallas_call(
        functools.partial(_prompt_kernel, rows=rows, chunk=chunk, n_t=n_t, n_blocks=n_blocks),
        grid=(n_blocks + 1,), in_specs=in_specs, out_specs=out_specs, out_shape=out_shape,
        scratch_shapes=scratch, compiler_params=_params(1), name="prompt_step",
    )(x.reshape(batch * seq, d), p.reshape(batch * seq, -1), *consts)
    return y.reshape(batch, seq, d), st, cs


def _sample(x, p, state, conv_state, mixer_consts, ffn_consts, *, mixer_rows, ffn_rows):
    batch, seq, d = x.shape
    width = ffn_consts[2].shape[1]
    gw = GROUPS * LANES
    rows = mixer_rows
    n_seq = rows // seq
    row_map = lambda i: (i, 0)
    st_spec = pl.BlockSpec((n_seq, HEADS, LANES, LANES), lambda i: (i, 0, 0, 0))
    h, st, v = pl.pallas_call(
        functools.partial(_mixer_sample_kernel, rows=rows, seq=seq),
        grid=(batch // n_seq,),
        in_specs=([pl.BlockSpec((rows, d), row_map)] + [_const_spec(a.shape) for a in mixer_consts]
                  + [st_spec]),
        out_specs=[pl.BlockSpec((rows, d), row_map), st_spec, pl.BlockSpec((rows, gw), row_map)],
        out_shape=[jax.ShapeDtypeStruct((batch * seq, d), F32),
                   jax.ShapeDtypeStruct(state.shape, F32),
                   jax.ShapeDtypeStruct((batch * seq, gw), F32)],
        compiler_params=_params(1), name="mixer_sample",
    )(x.reshape(batch * seq, d), *mixer_consts, state)
    rows = ffn_rows
    n_seq = rows // seq
    cs_spec = pl.BlockSpec((n_seq, CONV_W - 1, width), lambda i: (i, 0, 0))
    y, cs = pl.pallas_call(
        functools.partial(_ffn_sample_kernel, rows=rows, seq=seq),
        grid=(batch // n_seq,),
        in_specs=([pl.BlockSpec((rows, d), row_map), pl.BlockSpec((rows, p.shape[-1]), row_map)]
                  + [_const_spec(a.shape) for a in ffn_consts] + [cs_spec]),
        out_specs=[pl.BlockSpec((rows, d), row_map), cs_spec],
        out_shape=[jax.ShapeDtypeStruct((batch * seq, d), F32),
                   jax.ShapeDtypeStruct(conv_state.shape, F32)],
        compiler_params=_params(1), name="ffn_sample",
    )(h, p.reshape(batch * seq, -1), *ffn_consts, conv_state)
    return y.reshape(batch, seq, d), st, cs, v.reshape(batch, seq, gw)


def kernel(x_prompt, x_sample, p_prompt, p_sample, state_hgrn, state_conv, lb_logits, norm_mix_w, w_in, hgrn_norm_w, ln_v_w, ln_v_b, w_spatial, b_spatial, w_a_out, w_b_out, w_o, norm_ffn_w, w_up, conv_w, conv_b, w_down, norm_ple_w, w_ple_gate, w_ple_proj, final_norm_w):
    depth = w_in.shape[0]
    assert depth == 1 and lb_logits.shape[0] == 2
    seq_s = x_sample.shape[1]
    assert seq_s == SUBLANES and LANES % seq_s == 0
    l = 0
    row = lambda a: a.reshape(1, -1)

    ws, bs = w_spatial[l], b_spatial[l]
    reps = LANES // seq_s
    eye = jnp.eye(reps, dtype=F32)
    wmix_s = jnp.stack([jnp.kron(eye, ws[g, :seq_s, :seq_s]) for g in range(GROUPS)])
    bcol_s = jnp.tile(bs[:, :seq_s], (1, reps))[:, :, None]

    (w_in_b, w_a_b, w_b_b, w_o_b, w_up_b, w_down_b, w_pg_b, w_pp_b) = _to_bf16(
        [w_in[l], w_a_out[l], w_b_out[l], w_o[l], w_up[l], w_down[l], w_ple_gate[l], w_ple_proj[l]])

    def mixer_consts(wmix, bcol):
        return (lb_logits, row(norm_mix_w[l]), w_in_b, row(hgrn_norm_w[l]),
                row(ln_v_w[l]), row(ln_v_b[l]), wmix, bcol, w_a_b, w_b_b, w_o_b)

    ffn_consts = (row(norm_ffn_w[l]), w_up_b, conv_w[l], row(conv_b[l]), w_down_b,
                  row(norm_ple_w[l]), w_pg_b, w_pp_b, row(final_norm_w))

    y_p, hs_p, cs_p = _prompt(x_prompt, p_prompt[l], mixer_consts(ws, bs[:, :, None]), ffn_consts,
                              rows=256, chunk=64)
    y_s, hs_s, cs_s, v_s = _sample(x_sample, p_sample[l], state_hgrn[l], state_conv[l],
                                   mixer_consts(wmix_s, bcol_s), ffn_consts,
                                   mixer_rows=LANES, ffn_rows=256)
    return (y_p, y_s, hs_p[None], hs_s[None], cs_p[None], cs_s[None], v_s[None])
```

```python
import functools

import jax
import jax.numpy as jnp
from jax import lax
from jax.experimental import pallas as pl
from jax.experimental.pallas import tpu as pltpu

F32 = jnp.float32
BF16 = jnp.bfloat16
EPS = 1e-6

HEADS = 4
GROUPS = 4
CONV_W = 3
LANES = 128
SUBLANES = 8
FF_TILE = 256
CAST_STEPS = 8
VMEM_LIMIT = 56 * 1024 * 1024

MIXER_MATS = (2, 8, 9, 10)
FFN_MATS = (1, 4, 6, 7)

NT_DIMS = (((1,), (1,)), ((), ()))
TN_DIMS = (((0,), (0,)), ((), ()))


def _rms(x, w):
    return x * lax.rsqrt(jnp.mean(x * x, axis=-1, keepdims=True) + EPS) * w


def _gelu(x):
    return 0.5 * x * (1.0 + lax.erf(x * (2.0 ** -0.5)))


def _dot(a, b):
    return jnp.dot(a, b, preferred_element_type=F32)


def _hgrn_head(q, k, b, v, st_in, chunk, scan):
    rows = q.shape[0]
    n = rows // chunk
    shift = chunk.bit_length() - 1
    b3 = b.reshape(n, chunk, LANES)

    def chunk_row(r):
        return jnp.broadcast_to(b3[:, r:r + 1, :], b3.shape).reshape(rows, LANES)

    def blocked(a):
        cols = []
        for j in range(n):
            parts = [jnp.zeros((j * chunk, LANES), F32), a[j * chunk:(j + 1) * chunk],
                     jnp.zeros((rows - (j + 1) * chunk, LANES), F32)]
            cols.append(jnp.concatenate([p for p in parts if p.shape[0]], axis=0))
        return jnp.concatenate(cols, axis=1).astype(BF16)

    b_mid = chunk_row(chunk // 2 - 1)
    b_end = chunk_row(chunk - 1)
    q_t = (q * jnp.exp(b - b_mid)).astype(BF16)
    k_t = (k * jnp.exp(b_mid - b)).astype(BF16)
    vb = v.astype(BF16)
    k_blk = blocked(k * jnp.exp(b_end - b))
    q_blk = blocked(q * jnp.exp(b))
    yield
    scores = lax.dot_general(q_t, k_t, NT_DIMS, preferred_element_type=F32)
    ds = lax.dot_general(vb, k_blk, TN_DIMS, preferred_element_type=F32)
    yield
    row = lax.broadcasted_iota(jnp.int32, (rows, rows), 0)
    col = lax.broadcasted_iota(jnp.int32, (rows, rows), 1)
    keep = ((row >> shift) == (col >> shift)) & (col <= row)
    scores = jnp.where(keep, scores, 0.0).astype(BF16)
    before, after = [], []
    st = st_in
    for j in range(n):
        st_j = st if scan else st_in[j]
        before.append(st_j.astype(BF16))
        st = st_j * jnp.exp(b3[j, chunk - 1:chunk, :]) + ds[:, j * LANES:(j + 1) * LANES]
        after.append(st)
    s_rhs = jnp.concatenate(before, axis=1)
    yield
    o = _dot(scores, vb) + lax.dot_general(q_blk, s_rhs, NT_DIMS, preferred_element_type=F32)
    return o, after


def _interleave(streams, steps, lead=0):
    done = [False] * len(streams)
    for _ in range(lead):
        next(streams[0])
    while not all(done):
        for s, gen in enumerate(streams):
            for _ in range(steps[s]):
                if not done[s]:
                    try:
                        next(gen)
                    except StopIteration:
                        done[s] = True


def _proj_offsets(d_model):
    fdim = HEADS * LANES
    gw = GROUPS * LANES
    offs = [0]
    for width in (fdim, fdim, fdim, fdim, gw, gw, d_model, d_model):
        offs.append(offs[-1] + width)
    return offs


def _mixer_stream(x, wrefs, state_in, state_out, emit, *, rows, chunk, scan,
                  ready=lambda key: None):
    (lbl_ref, nmw_ref, win_ref, hnw_ref, lvw_ref, lvb_ref, wmix_ref, bcol_ref,
     wa_ref, wb_ref, wo_ref) = wrefs
    fdim = HEADS * LANES
    offs = _proj_offsets(x.shape[1])

    xn = _rms(x, nmw_ref[...]).astype(BF16)

    def proj(i):
        ready(("w_in", i))
        return _dot(xn, win_ref[:, offs[i]:offs[i + 1]])

    lbl = lbl_ref[...]
    lbe = jnp.exp(lbl - jnp.max(lbl, axis=0, keepdims=True))
    lb = lbe[0:1, :] / jnp.sum(lbe, axis=0, keepdims=True)

    fg = lb + (1.0 - lb) * jax.nn.sigmoid(proj(1))
    k = 1.0 - fg
    log_f = jnp.log(fg)
    lf_hi = log_f.astype(BF16)
    lf_rem = log_f - lf_hi.astype(F32)
    lf_mid = lf_rem.astype(BF16)
    lf_lo = (lf_rem - lf_mid.astype(F32)).astype(BF16)
    lf3 = jnp.concatenate([lf_hi, lf_mid, lf_lo], axis=1)
    yield
    q = jax.nn.silu(proj(0))
    yield
    v_in = proj(2)
    yield
    r128 = lax.broadcasted_iota(jnp.int32, (LANES, LANES), 0)
    c128 = lax.broadcasted_iota(jnp.int32, (LANES, LANES), 1)
    shift = min(chunk, LANES).bit_length() - 1
    seg = jnp.where(((r128 >> shift) == (c128 >> shift)) & (c128 <= r128), 1.0, 0.0).astype(BF16)
    b_rows = []
    for g in range(rows // LANES):
        b3 = _dot(seg, lf3[g * LANES:(g + 1) * LANES, :])
        b_rows.append(b3[:, :fdim] + b3[:, fdim:2 * fdim] + b3[:, 2 * fdim:])
    b = jnp.concatenate(b_rows, axis=0)
    yield

    def head(hd):
        cols = slice(hd * LANES, (hd + 1) * LANES)
        o, st_out = yield from _hgrn_head(q[:, cols], k[:, cols], b[:, cols], v_in[:, cols],
                                          state_in(hd), chunk, scan)
        state_out(hd, st_out)
        return o

    o_heads = [(yield from head(0))]
    yield
    og = jax.nn.sigmoid(proj(3))
    yield
    o_heads.append((yield from head(1)))
    yield
    u = _gelu(proj(4))
    yield
    o_heads.append((yield from head(2)))
    yield
    gv = _gelu(proj(5))
    gc = gv - jnp.mean(gv, axis=-1, keepdims=True)
    v = gc * lax.rsqrt(jnp.mean(gc * gc, axis=-1, keepdims=True) + EPS) * lvw_ref[...] + lvb_ref[...]
    emit("v", v)
    yield
    o_heads.append((yield from head(3)))
    yield
    gate_a = jax.nn.sigmoid(proj(6))
    yield
    gate_b = jax.nn.sigmoid(proj(7))
    yield
    o_a = _rms(jnp.concatenate(o_heads, axis=1) * og, hnw_ref[...]).astype(BF16)

    vb = v.astype(BF16)
    wm = [jnp.where(c128 <= r128, wmix_ref[g], 0.0).astype(BF16) for g in range(GROUPS)]
    bcol = [bcol_ref[g] for g in range(GROUPS)]
    n_rg = rows // LANES
    mixed_g = [
        _dot(wm[g], jnp.concatenate(
            [vb[rg * LANES:(rg + 1) * LANES, g * LANES:(g + 1) * LANES] for rg in range(n_rg)],
            axis=1)) for g in range(GROUPS)]
    ready("w_a")
    a_out = _dot(o_a, wa_ref[...])
    yield
    mixed = jnp.concatenate(
        [jnp.concatenate([mixed_g[g][:, rg * LANES:(rg + 1) * LANES] + bcol[g]
                          for g in range(GROUPS)], axis=1) for rg in range(n_rg)], axis=0)
    o_b = (u * mixed).astype(BF16)
    yield

    ready("w_b")
    mix = (gate_a * a_out + gate_b * _dot(o_b, wb_ref[...])).astype(BF16)
    yield
    ready("w_o")
    emit("h", x + _dot(mix, wo_ref[...]))


def _ffn_stream(h, p, wrefs, prev_rows, keep_rows, emit, *, rows, tpos):
    (nfw_ref, wup_ref, cw_ref, cb_ref, wdn_ref, npw_ref, wpg_ref, wpp_ref, fnw_ref) = wrefs
    d_ff = wdn_ref.shape[0]
    n_tiles = d_ff // FF_TILE
    hn = _rms(h, nfw_ref[...]).astype(BF16)

    def conv(up, lo):
        cols = slice(lo, lo + FF_TILE)
        w = cw_ref[:, cols]
        e0, e1 = prev_rows(cols)
        keep_rows(cols, up)
        x1 = jnp.where(tpos >= 1, pltpu.roll(up, 1, 0), e1)
        x2 = jnp.where(tpos >= 2, pltpu.roll(up, 2, 0), jnp.where(tpos == 1, e1, e0))
        return cb_ref[:, cols] + x2 * w[0:1, :] + x1 * w[1:2, :] + up * w[2:3, :]

    def up_pair(j):
        lo_a = j * FF_TILE
        lo_b = d_ff + j * FF_TILE
        return (_dot(hn, wup_ref[:, lo_a:lo_a + FF_TILE]), _dot(hn, wup_ref[:, lo_b:lo_b + FF_TILE]))

    acc = jnp.zeros((rows, h.shape[1]), F32)
    nxt = up_pair(0)
    pp = _dot(p.astype(BF16), wpp_ref[...])
    gated = None
    yield
    for j in range(n_tiles + 1):
        lo_a = j * FF_TILE
        cur = nxt
        if j + 1 < n_tiles:
            nxt = up_pair(j + 1)
        if j < n_tiles:
            ya = conv(cur[0], lo_a)
            yb = conv(cur[1], d_ff + lo_a)
            new_gated = (_gelu(ya) * yb).astype(BF16)
        if j > 0:
            acc = acc + _dot(gated, wdn_ref[lo_a - FF_TILE:lo_a, :])
        gated = new_gated
        yield
    h2 = h + acc
    gate = jax.nn.sigmoid(_dot(_rms(h2, npw_ref[...]).astype(BF16), wpg_ref[...]))
    yield
    h3 = h2 + gate * pp
    emit("y", _rms(h3, fnw_ref[...]))


def _prompt_kernel(*refs, rows, chunk, n_t, n_blocks):
    x_ref, p_ref = refs[:2]
    mixer_w = list(refs[2:13])
    ffn_w = list(refs[13:22])
    y_ref, sout_ref, cout_ref, st_s, carry_s, h_s = refs[22:28]
    wbufs, sem = refs[28:-1], refs[-1]
    i = pl.program_id(0)
    t_mix = lax.rem(i, n_t)
    t_ffn = lax.rem(jnp.maximum(i - 1, 0), n_t)

    copies = {}
    hbm_w = [mixer_w[k] for k in MIXER_MATS] + [ffn_w[k] for k in FFN_MATS]
    offs = _proj_offsets(x_ref.shape[1])
    for g in (1, 0, 2, 3, 4, 5, 6, 7):
        cols = slice(offs[g], offs[g + 1])
        copies[("w_in", g)] = (hbm_w[0].at[:, cols], wbufs[0].at[:, cols])
    for key, src, dst in zip(("w_a", "w_b", "w_o", "w_up", "w_down", "w_pg", "w_pp"),
                             hbm_w[1:], wbufs[1:]):
        copies[key] = (src, dst)
    order = list(copies)

    def copy(key):
        src, dst = copies[key]
        return pltpu.make_async_copy(src, dst, sem.at[order.index(key)])

    for k, buf in zip(MIXER_MATS, wbufs[:len(MIXER_MATS)]):
        mixer_w[k] = buf
    for k, buf in zip(FFN_MATS, wbufs[len(MIXER_MATS):]):
        ffn_w[k] = buf

    @pl.when(t_mix == 0)
    def _():
        st_s[...] = jnp.zeros_like(st_s)

    @pl.when(t_ffn == 0)
    def _():
        carry_s[...] = jnp.zeros_like(carry_s)

    slot = lax.rem(i, 2)

    def state_out(hd, states):
        st_s[hd] = states[-1]

    def prev_rows(cols):
        return carry_s[0:1, cols], carry_s[1:2, cols]

    def keep_rows(cols, up):
        carry_s[:, cols] = up[rows - 2:rows, :]
        cout_ref[0, :, cols] = up[rows - 2:rows, :]

    tpos = lax.broadcasted_iota(jnp.int32, (rows, FF_TILE), 0)

    def step(mixer, ffn, first=False):
        results = {}
        streams = []
        if first:
            for key in order:
                copy(key).start()
        if mixer:
            ready = (lambda key: copy(key).wait()) if first else (lambda key: None)
            streams.append(_mixer_stream(x_ref[...], mixer_w, lambda hd: st_s[hd], state_out,
                                         results.__setitem__, rows=rows, chunk=chunk, scan=True,
                                         ready=ready))
        if ffn:
            streams.append(_ffn_stream(h_s[1 - slot], p_ref[...], ffn_w, prev_rows, keep_rows,
                                       results.__setitem__, rows=rows, tpos=tpos))
        _interleave(streams, steps=(2, 1) if mixer and ffn else (1,), lead=1)
        if ffn:
            y_ref[...] = results["y"]
        if mixer:
            h_s[slot] = results["h"]
        if first:
            for key in ("w_up", "w_down", "w_pg", "w_pp"):
                copy(key).wait()

    pl.when(i == 0)(functools.partial(step, True, False, first=True))
    pl.when((i > 0) & (i < n_blocks))(functools.partial(step, True, True))
    pl.when(i == n_blocks)(functools.partial(step, False, True))

    @pl.when((t_mix == n_t - 1) & (i < n_blocks))
    def _():
        for hd in range(HEADS):
            sout_ref[0, hd] = st_s[hd].T


def _mixer_sample_kernel(*refs, rows, seq):
    x_ref = refs[0]
    wrefs = refs[1:12]
    sin_ref, h_ref, sout_ref, vout_ref = refs[12:]
    n_seq = rows // seq
    results = {}

    def state_out(hd, states):
        for j in range(n_seq):
            sout_ref[j, hd] = states[j].T

    for _ in _mixer_stream(x_ref[...], wrefs,
                           lambda hd: [sin_ref[j, hd].T for j in range(n_seq)], state_out,
                           results.__setitem__, rows=rows, chunk=seq, scan=False):
        pass
    vout_ref[...] = results["v"]
    h_ref[...] = results["h"]


def _ffn_sample_kernel(*refs, rows, seq):
    h_ref, p_ref = refs[:2]
    wrefs = refs[2:11]
    cst_ref, y_ref, cout_ref = refs[11:]
    n_seq = rows // seq
    results = {}

    def prev_rows(cols):
        st = cst_ref[:, :, cols]
        width = st.shape[-1]
        return tuple(jnp.broadcast_to(st[:, r:r + 1, :], (n_seq, seq, width)).reshape(rows, width)
                     for r in range(CONV_W - 1))

    def keep_rows(cols, up):
        cout_ref[:, :, cols] = up.reshape(n_seq, seq, up.shape[-1])[:, seq - (CONV_W - 1):seq, :]

    tpos = lax.broadcasted_iota(jnp.int32, (rows, FF_TILE), 0) & (seq - 1)
    for _ in _ffn_stream(h_ref[...], p_ref[...], wrefs, prev_rows, keep_rows, results.__setitem__,
                         rows=rows, tpos=tpos):
        pass
    y_ref[...] = results["y"]


def _const_spec(shape):
    nd = len(shape)
    return pl.BlockSpec(shape, lambda *_: (0,) * nd, pipeline_mode=pl.Buffered(1))


def _params(n_axes):
    return pltpu.CompilerParams(dimension_semantics=("arbitrary",) * n_axes,
                                vmem_limit_bytes=VMEM_LIMIT)


def _cast_kernel(*refs):
    n = len(refs) // 2
    for src, dst in zip(refs[:n], refs[n:]):
        dst[...] = src[...].astype(dst.dtype)


def _to_bf16(mats):
    tile_rows = 2 * SUBLANES
    for m in mats:
        assert m.shape[0] % (CAST_STEPS * tile_rows) == 0 and m.shape[1] % LANES == 0
    specs = [pl.BlockSpec((m.shape[0] // CAST_STEPS, m.shape[1]), lambda i: (i, 0)) for m in mats]
    return pl.pallas_call(
        _cast_kernel, grid=(CAST_STEPS,), in_specs=specs, out_specs=specs,
        out_shape=[jax.ShapeDtypeStruct(m.shape, BF16) for m in mats],
        compiler_params=_params(1), name="weights_to_bf16",
    )(*mats)


def _prompt(x, p, mixer_consts, ffn_consts, *, rows, chunk):
    batch, seq, d = x.shape
    n_t = seq // rows
    n_blocks = batch * n_t
    width = ffn_consts[2].shape[1]
    mix_blk = lambda i: jnp.minimum(i, n_blocks - 1)
    ffn_blk = lambda i: jnp.maximum(i - 1, 0)
    consts = tuple(mixer_consts) + tuple(ffn_consts)
    mats = list(MIXER_MATS) + [len(mixer_consts) + k for k in FFN_MATS]
    in_specs = ([pl.BlockSpec((rows, d), lambda i: (mix_blk(i), 0)),
                 pl.BlockSpec((rows, p.shape[-1]), lambda i: (ffn_blk(i), 0))]
                + [pl.BlockSpec(memory_space=pl.ANY) if k in mats else _const_spec(a.shape)
                   for k, a in enumerate(consts)])
    out_specs = [pl.BlockSpec((rows, d), lambda i: (ffn_blk(i), 0)),
                 pl.BlockSpec((1, HEADS, LANES, LANES), lambda i: (mix_blk(i) // n_t, 0, 0, 0)),
                 pl.BlockSpec((1, CONV_W - 1, width), lambda i: (ffn_blk(i) // n_t, 0, 0))]
    out_shape = [jax.ShapeDtypeStruct((batch * seq, d), F32),
                 jax.ShapeDtypeStruct((batch, HEADS, LANES, LANES), F32),
                 jax.ShapeDtypeStruct((batch, CONV_W - 1, width), F32)]
    scratch = [pltpu.VMEM((HEADS, LANES, LANES), F32), pltpu.VMEM((CONV_W - 1, width), F32),
               pltpu.VMEM((2, rows, d), F32)]
    scratch += [pltpu.VMEM(consts[k].shape, BF16) for k in mats]
    scratch.append(pltpu.SemaphoreType.DMA((len(_proj_offsets(d)) - 1 + len(mats) - 1,)))
    y, st, cs = pl.pallas_call(
        functools.partial(_prompt_kernel, rows=rows, chunk=chunk, n_t=n_t, n_blocks=n_blocks),
        grid=(n_blocks + 1,), in_specs=in_specs, out_specs=out_specs, out_shape=out_shape,
        scratch_shapes=scratch, compiler_params=_params(1), name="prompt_step",
    )(x.reshape(batch * seq, d), p.reshape(batch * seq, -1), *consts)
    return y.reshape(batch, seq, d), st, cs


def _sample(x, p, state, conv_state, mixer_consts, ffn_consts, *, mixer_rows, ffn_rows):
    batch, seq, d = x.shape
    width = ffn_consts[2].shape[1]
    gw = GROUPS * LANES
    rows = mixer_rows
    n_seq = rows // seq
    row_map = lambda i: (i, 0)
    st_spec = pl.BlockSpec((n_seq, HEADS, LANES, LANES), lambda i: (i, 0, 0, 0))
    h, st, v = pl.pallas_call(
        functools.partial(_mixer_sample_kernel, rows=rows, seq=seq),
        grid=(batch // n_seq,),
        in_specs=([pl.BlockSpec((rows, d), row_map)] + [_const_spec(a.shape) for a in mixer_consts]
                  + [st_spec]),
        out_specs=[pl.BlockSpec((rows, d), row_map), st_spec, pl.BlockSpec((rows, gw), row_map)],
        out_shape=[jax.ShapeDtypeStruct((batch * seq, d), F32),
                   jax.ShapeDtypeStruct(state.shape, F32),
                   jax.ShapeDtypeStruct((batch * seq, gw), F32)],
        compiler_params=_params(1), name="mixer_sample",
    )(x.reshape(batch * seq, d), *mixer_consts, state)
    rows = ffn_rows
    n_seq = rows // seq
    cs_spec = pl.BlockSpec((n_seq, CONV_W - 1, width), lambda i: (i, 0, 0))
    y, cs = pl.pallas_call(
        functools.partial(_ffn_sample_kernel, rows=rows, seq=seq),
        grid=(batch // n_seq,),
        in_specs=([pl.BlockSpec((rows, d), row_map), pl.BlockSpec((rows, p.shape[-1]), row_map)]
                  + [_const_spec(a.shape) for a in ffn_consts] + [cs_spec]),
        out_specs=[pl.BlockSpec((rows, d), row_map), cs_spec],
        out_shape=[jax.ShapeDtypeStruct((batch * seq, d), F32),
                   jax.ShapeDtypeStruct(conv_state.shape, F32)],
        compiler_params=_params(1), name="ffn_sample",
    )(h, p.reshape(batch * seq, -1), *ffn_consts, conv_state)
    return y.reshape(batch, seq, d), st, cs, v.reshape(batch, seq, gw)


def kernel(x_prompt, x_sample, p_prompt, p_sample, state_hgrn, state_conv, lb_logits, norm_mix_w, w_in, hgrn_norm_w, ln_v_w, ln_v_b, w_spatial, b_spatial, w_a_out, w_b_out, w_o, norm_ffn_w, w_up, conv_w, conv_b, w_down, norm_ple_w, w_ple_gate, w_ple_proj, final_norm_w):
    depth = w_in.shape[0]
    assert depth == 1 and lb_logits.shape[0] == 2
    seq_s = x_sample.shape[1]
    assert seq_s == SUBLANES and LANES % seq_s == 0
    l = 0
    row = lambda a: a.reshape(1, -1)

    ws, bs = w_spatial[l], b_spatial[l]
    reps = LANES // seq_s
    eye = jnp.eye(reps, dtype=F32)
    wmix_s = jnp.stack([jnp.kron(eye, ws[g, :seq_s, :seq_s]) for g in range(GROUPS)])
    bcol_s = jnp.tile(bs[:, :seq_s], (1, reps))[:, :, None]

    (w_in_b, w_a_b, w_b_b, w_o_b, w_up_b, w_down_b, w_pg_b, w_pp_b) = _to_bf16(
        [w_in[l], w_a_out[l], w_b_out[l], w_o[l], w_up[l], w_down[l], w_ple_gate[l], w_ple_proj[l]])

    def mixer_consts(wmix, bcol):
        return (lb_logits, row(norm_mix_w[l]), w_in_b, row(hgrn_norm_w[l]),
                row(ln_v_w[l]), row(ln_v_b[l]), wmix, bcol, w_a_b, w_b_b, w_o_b)

    ffn_consts = (row(norm_ffn_w[l]), w_up_b, conv_w[l], row(conv_b[l]), w_down_b,
                  row(norm_ple_w[l]), w_pg_b, w_pp_b, row(final_norm_w))

    y_p, hs_p, cs_p = _prompt(x_prompt, p_prompt[l], mixer_consts(ws, bs[:, :, None]), ffn_consts,
                              rows=256, chunk=64)
    y_s, hs_s, cs_s, v_s = _sample(x_sample, p_sample[l], state_hgrn[l], state_conv[l],
                                   mixer_consts(wmix_s, bcol_s), ffn_consts,
                                   mixer_rows=LANES, ffn_rows=256)
    return (y_p, y_s, hs_p[None], hs_s[None], cs_p[None], cs_s[None], v_s[None])
```

```python
import functools

import jax
import jax.numpy as jnp
from jax import lax
from jax.experimental import pallas as pl
from jax.experimental.pallas import tpu as pltpu

F32 = jnp.float32
BF16 = jnp.bfloat16
EPS = 1e-6

HEADS = 4
GROUPS = 4
CONV_W = 3
LANES = 128
SUBLANES = 8
FF_TILE = 256
CAST_STEPS = 8
BLOCK_ROWS = 256
HGRN_CHUNK = 64
PHASES = (2, 1)
VMEM_LIMIT = 56 * 1024 * 1024

NT_DIMS = (((1,), (1,)), ((), ()))
TN_DIMS = (((0,), (0,)), ((), ()))


def _rms(x, w):
    return x * lax.rsqrt(jnp.mean(x * x, axis=-1, keepdims=True) + EPS) * w


def _gelu(x):
    return 0.5 * x * (1.0 + lax.erf(x * (2.0 ** -0.5)))


def _dot(a, b):
    return jnp.dot(a, b, preferred_element_type=F32)


def _hgrn_head(q, k, b, v, st_in, chunk, scan):
    rows = q.shape[0]
    n = rows // chunk
    shift = chunk.bit_length() - 1
    b3 = b.reshape(n, chunk, LANES)

    def chunk_row(r):
        return jnp.broadcast_to(b3[:, r:r + 1, :], b3.shape).reshape(rows, LANES)

    def blocked(a):
        cols = []
        for j in range(n):
            parts = [jnp.zeros((j * chunk, LANES), F32), a[j * chunk:(j + 1) * chunk],
                     jnp.zeros((rows - (j + 1) * chunk, LANES), F32)]
            cols.append(jnp.concatenate([p for p in parts if p.shape[0]], axis=0))
        return jnp.concatenate(cols, axis=1).astype(BF16)

    b_mid = chunk_row(chunk // 2 - 1)
    b_end = chunk_row(chunk - 1)
    q_t = (q * jnp.exp(b - b_mid)).astype(BF16)
    k_t = (k * jnp.exp(b_mid - b)).astype(BF16)
    vb = v.astype(BF16)
    k_blk = blocked(k * jnp.exp(b_end - b))
    q_blk = blocked(q * jnp.exp(b))
    yield
    scores = lax.dot_general(q_t, k_t, NT_DIMS, preferred_element_type=F32)
    ds = lax.dot_general(vb, k_blk, TN_DIMS, preferred_element_type=F32)
    yield
    row = lax.broadcasted_iota(jnp.int32, (rows, rows), 0)
    col = lax.broadcasted_iota(jnp.int32, (rows, rows), 1)
    keep = ((row >> shift) == (col >> shift)) & (col <= row)
    scores = jnp.where(keep, scores, 0.0).astype(BF16)
    before, after = [], []
    st = st_in
    for j in range(n):
        st_j = st if scan else st_in[j]
        before.append(st_j.astype(BF16))
        st = st_j * jnp.exp(b3[j, chunk - 1:chunk, :]) + ds[:, j * LANES:(j + 1) * LANES]
        after.append(st)
    s_rhs = jnp.concatenate(before, axis=1)
    yield
    o = _dot(scores, vb) + lax.dot_general(q_blk, s_rhs, NT_DIMS, preferred_element_type=F32)
    return o, after


def _interleave(streams, steps, lead=0):
    done = [False] * len(streams)
    for _ in range(lead):
        next(streams[0])
    while not all(done):
        for s, gen in enumerate(streams):
            for _ in range(steps[s]):
                if not done[s]:
                    try:
                        next(gen)
                    except StopIteration:
                        done[s] = True


def _mixer_stream(x, wrefs, state_in, state_out, emit, *, rows, chunk, scan):
    (lbl_ref, nmw_ref, win_ref, hnw_ref, lvw_ref, lvb_ref, wmix_ref, bcol_ref,
     wa_ref, wb_ref, wo_ref) = wrefs
    fdim = HEADS * LANES
    gw = GROUPS * LANES
    d_model = x.shape[1]
    offs = [0]
    for width in (fdim, fdim, fdim, fdim, gw, gw, d_model, d_model):
        offs.append(offs[-1] + width)

    xn = _rms(x, nmw_ref[...]).astype(BF16)

    def proj(i):
        return _dot(xn, win_ref[:, offs[i]:offs[i + 1]])

    lbl = lbl_ref[...]
    lbe = jnp.exp(lbl - jnp.max(lbl, axis=0, keepdims=True))
    lb = lbe[0:1, :] / jnp.sum(lbe, axis=0, keepdims=True)

    fg = lb + (1.0 - lb) * jax.nn.sigmoid(proj(1))
    k = 1.0 - fg
    log_f = jnp.log(fg)
    lf_hi = log_f.astype(BF16)
    lf_rem = log_f - lf_hi.astype(F32)
    lf_mid = lf_rem.astype(BF16)
    lf_lo = (lf_rem - lf_mid.astype(F32)).astype(BF16)
    lf3 = jnp.concatenate([lf_hi, lf_mid, lf_lo], axis=1)
    yield
    q = jax.nn.silu(proj(0))
    yield
    v_in = proj(2)
    yield
    r128 = lax.broadcasted_iota(jnp.int32, (LANES, LANES), 0)
    c128 = lax.broadcasted_iota(jnp.int32, (LANES, LANES), 1)
    shift = min(chunk, LANES).bit_length() - 1
    seg = jnp.where(((r128 >> shift) == (c128 >> shift)) & (c128 <= r128), 1.0, 0.0).astype(BF16)
    b_rows = []
    for g in range(rows // LANES):
        b3 = _dot(seg, lf3[g * LANES:(g + 1) * LANES, :])
        b_rows.append(b3[:, :fdim] + b3[:, fdim:2 * fdim] + b3[:, 2 * fdim:])
    b = jnp.concatenate(b_rows, axis=0)
    yield

    def head(hd):
        cols = slice(hd * LANES, (hd + 1) * LANES)
        o, st_out = yield from _hgrn_head(q[:, cols], k[:, cols], b[:, cols], v_in[:, cols],
                                          state_in(hd), chunk, scan)
        state_out(hd, st_out)
        return o

    o_heads = [(yield from head(0))]
    yield
    og = jax.nn.sigmoid(proj(3))
    yield
    o_heads.append((yield from head(1)))
    yield
    u = _gelu(proj(4))
    yield
    o_heads.append((yield from head(2)))
    yield
    gv = _gelu(proj(5))
    gc = gv - jnp.mean(gv, axis=-1, keepdims=True)
    v = gc * lax.rsqrt(jnp.mean(gc * gc, axis=-1, keepdims=True) + EPS) * lvw_ref[...] + lvb_ref[...]
    emit("v", v)
    yield
    o_heads.append((yield from head(3)))
    yield
    gate_a = jax.nn.sigmoid(proj(6))
    yield
    gate_b = jax.nn.sigmoid(proj(7))
    yield
    o_a = _rms(jnp.concatenate(o_heads, axis=1) * og, hnw_ref[...]).astype(BF16)

    vb = v.astype(BF16)
    wm = [jnp.where(c128 <= r128, wmix_ref[g], 0.0).astype(BF16) for g in range(GROUPS)]
    bcol = [bcol_ref[g] for g in range(GROUPS)]
    n_rg = rows // LANES
    mixed_g = [
        _dot(wm[g], jnp.concatenate(
            [vb[rg * LANES:(rg + 1) * LANES, g * LANES:(g + 1) * LANES] for rg in range(n_rg)],
            axis=1)) for g in range(GROUPS)]
    a_out = _dot(o_a, wa_ref[...])
    yield
    mixed = jnp.concatenate(
        [jnp.concatenate([mixed_g[g][:, rg * LANES:(rg + 1) * LANES] + bcol[g]
                          for g in range(GROUPS)], axis=1) for rg in range(n_rg)], axis=0)
    o_b = (u * mixed).astype(BF16)
    yield

    mix = (gate_a * a_out + gate_b * _dot(o_b, wb_ref[...])).astype(BF16)
    yield
    emit("h", x + _dot(mix, wo_ref[...]))


def _ffn_stream(h, p, wrefs, prev_rows, keep_rows, emit, *, rows, tpos):
    (nfw_ref, wup_ref, cw_ref, cb_ref, wdn_ref, npw_ref, wpg_ref, wpp_ref, fnw_ref) = wrefs
    d_ff = wdn_ref.shape[0]
    n_tiles = d_ff // FF_TILE
    hn = _rms(h, nfw_ref[...]).astype(BF16)

    def conv(up, lo):
        cols = slice(lo, lo + FF_TILE)
        w = cw_ref[:, cols]
        e0, e1 = prev_rows(cols)
        keep_rows(cols, up)
        x1 = jnp.where(tpos >= 1, pltpu.roll(up, 1, 0), e1)
        x2 = jnp.where(tpos >= 2, pltpu.roll(up, 2, 0), jnp.where(tpos == 1, e1, e0))
        return cb_ref[:, cols] + x2 * w[0:1, :] + x1 * w[1:2, :] + up * w[2:3, :]

    def up_pair(j):
        lo_a = j * FF_TILE
        lo_b = d_ff + j * FF_TILE
        return (_dot(hn, wup_ref[:, lo_a:lo_a + FF_TILE]), _dot(hn, wup_ref[:, lo_b:lo_b + FF_TILE]))

    acc = jnp.zeros((rows, h.shape[1]), F32)
    nxt = up_pair(0)
    pp = _dot(p.astype(BF16), wpp_ref[...])
    gated = None
    yield
    for j in range(n_tiles + 1):
        lo_a = j * FF_TILE
        cur = nxt
        if j + 1 < n_tiles:
            nxt = up_pair(j + 1)
        if j < n_tiles:
            ya = conv(cur[0], lo_a)
            yb = conv(cur[1], d_ff + lo_a)
            new_gated = (_gelu(ya) * yb).astype(BF16)
        if j > 0:
            acc = acc + _dot(gated, wdn_ref[lo_a - FF_TILE:lo_a, :])
        gated = new_gated
        yield
    h2 = h + acc
    gate = jax.nn.sigmoid(_dot(_rms(h2, npw_ref[...]).astype(BF16), wpg_ref[...]))
    yield
    h3 = h2 + gate * pp
    emit("y", _rms(h3, fnw_ref[...]))


def _prompt_kernel(*refs, rows, chunk, n_t, n_blocks):
    x_ref, p_ref = refs[:2]
    mixer_w = refs[2:13]
    ffn_w = refs[13:22]
    y_ref, sout_ref, cout_ref, st_s, carry_s, h_s = refs[22:]
    i = pl.program_id(0)
    t_mix = lax.rem(i, n_t)
    t_ffn = lax.rem(jnp.maximum(i - 1, 0), n_t)

    @pl.when(t_mix == 0)
    def _():
        st_s[...] = jnp.zeros_like(st_s)

    @pl.when(t_ffn == 0)
    def _():
        carry_s[...] = jnp.zeros_like(carry_s)

    slot = lax.rem(i, 2)

    def state_out(hd, states):
        st_s[hd] = states[-1]

    def prev_rows(cols):
        return carry_s[0:1, cols], carry_s[1:2, cols]

    def keep_rows(cols, up):
        carry_s[:, cols] = up[rows - 2:rows, :]
        cout_ref[0, :, cols] = up[rows - 2:rows, :]

    tpos = lax.broadcasted_iota(jnp.int32, (rows, FF_TILE), 0)

    def step(mixer, ffn):
        results = {}
        streams = []
        if mixer:
            streams.append(_mixer_stream(x_ref[...], mixer_w, lambda hd: st_s[hd], state_out,
                                         results.__setitem__, rows=rows, chunk=chunk, scan=True))
        if ffn:
            streams.append(_ffn_stream(h_s[1 - slot], p_ref[...], ffn_w, prev_rows, keep_rows,
                                       results.__setitem__, rows=rows, tpos=tpos))
        _interleave(streams, steps=PHASES if mixer and ffn else (1,), lead=1)
        if ffn:
            y_ref[...] = results["y"]
        if mixer:
            h_s[slot] = results["h"]

    pl.when(i == 0)(functools.partial(step, True, False))
    pl.when((i > 0) & (i < n_blocks))(functools.partial(step, True, True))
    pl.when(i == n_blocks)(functools.partial(step, False, True))

    @pl.when((t_mix == n_t - 1) & (i < n_blocks))
    def _():
        for hd in range(HEADS):
            sout_ref[0, hd] = st_s[hd].T


def _mixer_sample_kernel(*refs, rows, seq):
    x_ref = refs[0]
    wrefs = refs[1:12]
    sin_ref, h_ref, sout_ref, vout_ref = refs[12:]
    n_seq = rows // seq
    results = {}

    def state_out(hd, states):
        for j in range(n_seq):
            sout_ref[j, hd] = states[j].T

    for _ in _mixer_stream(x_ref[...], wrefs,
                           lambda hd: [sin_ref[j, hd].T for j in range(n_seq)], state_out,
                           results.__setitem__, rows=rows, chunk=seq, scan=False):
        pass
    vout_ref[...] = results["v"]
    h_ref[...] = results["h"]


def _ffn_sample_kernel(*refs, rows, seq):
    h_ref, p_ref = refs[:2]
    wrefs = refs[2:11]
    cst_ref, y_ref, cout_ref = refs[11:]
    n_seq = rows // seq
    results = {}

    def prev_rows(cols):
        st = cst_ref[:, :, cols]
        width = st.shape[-1]
        return tuple(jnp.broadcast_to(st[:, r:r + 1, :], (n_seq, seq, width)).reshape(rows, width)
                     for r in range(CONV_W - 1))

    def keep_rows(cols, up):
        cout_ref[:, :, cols] = up.reshape(n_seq, seq, up.shape[-1])[:, seq - (CONV_W - 1):seq, :]

    tpos = lax.broadcasted_iota(jnp.int32, (rows, FF_TILE), 0) & (seq - 1)
    for _ in _ffn_stream(h_ref[...], p_ref[...], wrefs, prev_rows, keep_rows, results.__setitem__,
                         rows=rows, tpos=tpos):
        pass
    y_ref[...] = results["y"]


def _const_spec(shape):
    nd = len(shape)
    return pl.BlockSpec(shape, lambda *_: (0,) * nd, pipeline_mode=pl.Buffered(1))


def _params(n_axes):
    return pltpu.CompilerParams(dimension_semantics=("arbitrary",) * n_axes,
                                vmem_limit_bytes=VMEM_LIMIT)


def _cast_kernel(*refs):
    n = len(refs) // 2
    for src, dst in zip(refs[:n], refs[n:]):
        dst[...] = src[...].astype(dst.dtype)


def _to_bf16(mats):
    tile_rows = 2 * SUBLANES
    for m in mats:
        assert m.shape[0] % (CAST_STEPS * tile_rows) == 0 and m.shape[1] % LANES == 0
    specs = [pl.BlockSpec((m.shape[0] // CAST_STEPS, m.shape[1]), lambda i: (i, 0)) for m in mats]
    return pl.pallas_call(
        _cast_kernel, grid=(CAST_STEPS,), in_specs=specs, out_specs=specs,
        out_shape=[jax.ShapeDtypeStruct(m.shape, BF16) for m in mats],
        compiler_params=_params(1), name="weights_to_bf16",
    )(*mats)


def _prompt(x, p, mixer_consts, ffn_consts, *, rows, chunk):
    batch, seq, d = x.shape
    n_t = seq // rows
    n_blocks = batch * n_t
    width = ffn_consts[2].shape[1]
    mix_blk = lambda i: jnp.minimum(i, n_blocks - 1)
    ffn_blk = lambda i: jnp.maximum(i - 1, 0)
    consts = tuple(mixer_consts) + tuple(ffn_consts)
    in_specs = ([pl.BlockSpec((rows, d), lambda i: (mix_blk(i), 0)),
                 pl.BlockSpec((rows, p.shape[-1]), lambda i: (ffn_blk(i), 0))]
                + [_const_spec(a.shape) for a in consts])
    out_specs = [pl.BlockSpec((rows, d), lambda i: (ffn_blk(i), 0)),
                 pl.BlockSpec((1, HEADS, LANES, LANES), lambda i: (mix_blk(i) // n_t, 0, 0, 0)),
                 pl.BlockSpec((1, CONV_W - 1, width), lambda i: (ffn_blk(i) // n_t, 0, 0))]
    out_shape = [jax.ShapeDtypeStruct((batch * seq, d), F32),
                 jax.ShapeDtypeStruct((batch, HEADS, LANES, LANES), F32),
                 jax.ShapeDtypeStruct((batch, CONV_W - 1, width), F32)]
    scratch = [pltpu.VMEM((HEADS, LANES, LANES), F32), pltpu.VMEM((CONV_W - 1, width), F32),
               pltpu.VMEM((2, rows, d), F32)]
    y, st, cs = pl.pallas_call(
        functools.partial(_prompt_kernel, rows=rows, chunk=chunk, n_t=n_t, n_blocks=n_blocks),
        grid=(n_blocks + 1,), in_specs=in_specs, out_specs=out_specs, out_shape=out_shape,
        scratch_shapes=scratch, compiler_params=_params(1), name="prompt_step",
    )(x.reshape(batch * seq, d), p.reshape(batch * seq, -1), *consts)
    return y.reshape(batch, seq, d), st, cs


def _sample(x, p, state, conv_state, mixer_consts, ffn_consts, *, mixer_rows, ffn_rows):
    batch, seq, d = x.shape
    width = ffn_consts[2].shape[1]
    gw = GROUPS * LANES
    rows = mixer_rows
    n_seq = rows // seq
    row_map = lambda i: (i, 0)
    st_spec = pl.BlockSpec((n_seq, HEADS, LANES, LANES), lambda i: (i, 0, 0, 0))
    h, st, v = pl.pallas_call(
        functools.partial(_mixer_sample_kernel, rows=rows, seq=seq),
        grid=(batch // n_seq,),
        in_specs=([pl.BlockSpec((rows, d), row_map)] + [_const_spec(a.shape) for a in mixer_consts]
                  + [st_spec]),
        out_specs=[pl.BlockSpec((rows, d), row_map), st_spec, pl.BlockSpec((rows, gw), row_map)],
        out_shape=[jax.ShapeDtypeStruct((batch * seq, d), F32),
                   jax.ShapeDtypeStruct(state.shape, F32),
                   jax.ShapeDtypeStruct((batch * seq, gw), F32)],
        compiler_params=_params(1), name="mixer_sample",
    )(x.reshape(batch * seq, d), *mixer_consts, state)
    rows = ffn_rows
    n_seq = rows // seq
    cs_spec = pl.BlockSpec((n_seq, CONV_W - 1, width), lambda i: (i, 0, 0))
    y, cs = pl.pallas_call(
        functools.partial(_ffn_sample_kernel, rows=rows, seq=seq),
        grid=(batch // n_seq,),
        in_specs=([pl.BlockSpec((rows, d), row_map), pl.BlockSpec((rows, p.shape[-1]), row_map)]
                  + [_const_spec(a.shape) for a in ffn_consts] + [cs_spec]),
        out_specs=[pl.BlockSpec((rows, d), row_map), cs_spec],
        out_shape=[jax.ShapeDtypeStruct((batch * seq, d), F32),
                   jax.ShapeDtypeStruct(conv_state.shape, F32)],
        compiler_params=_params(1), name="ffn_sample",
    )(h, p.reshape(batch * seq, -1), *ffn_consts, conv_state)
    return y.reshape(batch, seq, d), st, cs, v.reshape(batch, seq, gw)


def kernel(x_prompt, x_sample, p_prompt, p_sample, state_hgrn, state_conv, lb_logits, norm_mix_w, w_in, hgrn_norm_w, ln_v_w, ln_v_b, w_spatial, b_spatial, w_a_out, w_b_out, w_o, norm_ffn_w, w_up, conv_w, conv_b, w_down, norm_ple_w, w_ple_gate, w_ple_proj, final_norm_w):
    depth = w_in.shape[0]
    assert depth == 1 and lb_logits.shape[0] == 2
    seq_s = x_sample.shape[1]
    assert seq_s == SUBLANES and LANES % seq_s == 0
    l = 0
    row = lambda a: a.reshape(1, -1)

    ws, bs = w_spatial[l], b_spatial[l]
    reps = LANES // seq_s
    eye = jnp.eye(reps, dtype=F32)
    wmix_s = jnp.stack([jnp.kron(eye, ws[g, :seq_s, :seq_s]) for g in range(GROUPS)])
    bcol_s = jnp.tile(bs[:, :seq_s], (1, reps))[:, :, None]

    (w_in_b, w_a_b, w_b_b, w_o_b, w_up_b, w_down_b, w_pg_b, w_pp_b) = _to_bf16(
        [w_in[l], w_a_out[l], w_b_out[l], w_o[l], w_up[l], w_down[l], w_ple_gate[l], w_ple_proj[l]])

    def mixer_consts(wmix, bcol):
        return (lb_logits, row(norm_mix_w[l]), w_in_b, row(hgrn_norm_w[l]),
                row(ln_v_w[l]), row(ln_v_b[l]), wmix, bcol, w_a_b, w_b_b, w_o_b)

    ffn_consts = (row(norm_ffn_w[l]), w_up_b, conv_w[l], row(conv_b[l]), w_down_b,
                  row(norm_ple_w[l]), w_pg_b, w_pp_b, row(final_norm_w))

    y_p, hs_p, cs_p = _prompt(x_prompt, p_prompt[l], mixer_consts(ws, bs[:, :, None]), ffn_consts,
                              rows=BLOCK_ROWS, chunk=HGRN_CHUNK)
    y_s, hs_s, cs_s, v_s = _sample(x_sample, p_sample[l], state_hgrn[l], state_conv[l],
                                   mixer_consts(wmix_s, bcol_s), ffn_consts,
                                   mixer_rows=LANES, ffn_rows=BLOCK_ROWS)
    return (y_p, y_s, hs_p[None], hs_s[None], cs_p[None], cs_s[None], v_s[None])
```

```python
import functools

import jax
import jax.numpy as jnp
from jax import lax
from jax.experimental import pallas as pl
from jax.experimental.pallas import tpu as pltpu

F32 = jnp.float32
BF16 = jnp.bfloat16
EPS = 1e-6

HEADS = 4
GROUPS = 4
CONV_W = 3
LANES = 128
SUBLANES = 8
FF_TILE = 256
CAST_STEPS = 8
BLOCK_ROWS = 256
HGRN_CHUNK = 64
PHASES = (2, 1)
VMEM_LIMIT = 56 * 1024 * 1024

NT_DIMS = (((1,), (1,)), ((), ()))
TN_DIMS = (((0,), (0,)), ((), ()))


def _rms(x, w):
    return x * lax.rsqrt(jnp.mean(x * x, axis=-1, keepdims=True) + EPS) * w


def _gelu(x):
    return 0.5 * x * (1.0 + lax.erf(x * (2.0 ** -0.5)))


def _dot(a, b):
    return jnp.dot(a, b, preferred_element_type=F32)


def _hgrn_head(q, k, b, v, st_in, chunk, scan):
    rows = q.shape[0]
    n = rows // chunk
    shift = chunk.bit_length() - 1
    b3 = b.reshape(n, chunk, LANES)

    def chunk_row(r):
        return jnp.broadcast_to(b3[:, r:r + 1, :], b3.shape).reshape(rows, LANES)

    def blocked(a):
        cols = []
        for j in range(n):
            parts = [jnp.zeros((j * chunk, LANES), F32), a[j * chunk:(j + 1) * chunk],
                     jnp.zeros((rows - (j + 1) * chunk, LANES), F32)]
            cols.append(jnp.concatenate([p for p in parts if p.shape[0]], axis=0))
        return jnp.concatenate(cols, axis=1).astype(BF16)

    b_mid = chunk_row(chunk // 2 - 1)
    b_end = chunk_row(chunk - 1)
    q_t = (q * jnp.exp(b - b_mid)).astype(BF16)
    k_t = (k * jnp.exp(b_mid - b)).astype(BF16)
    vb = v.astype(BF16)
    k_blk = blocked(k * jnp.exp(b_end - b))
    q_blk = blocked(q * jnp.exp(b))
    yield
    scores = lax.dot_general(q_t, k_t, NT_DIMS, preferred_element_type=F32)
    ds = lax.dot_general(vb, k_blk, TN_DIMS, preferred_element_type=F32)
    yield
    row = lax.broadcasted_iota(jnp.int32, (rows, rows), 0)
    col = lax.broadcasted_iota(jnp.int32, (rows, rows), 1)
    keep = ((row >> shift) == (col >> shift)) & (col <= row)
    scores = jnp.where(keep, scores, 0.0).astype(BF16)
    before, after = [], []
    st = st_in
    for j in range(n):
        st_j = st if scan else st_in[j]
        before.append(st_j.astype(BF16))
        st = st_j * jnp.exp(b3[j, chunk - 1:chunk, :]) + ds[:, j * LANES:(j + 1) * LANES]
        after.append(st)
    s_rhs = jnp.concatenate(before, axis=1)
    yield
    o = _dot(scores, vb) + lax.dot_general(q_blk, s_rhs, NT_DIMS, preferred_element_type=F32)
    return o, after


def _interleave(streams, steps, lead=0):
    done = [False] * len(streams)
    for _ in range(lead):
        next(streams[0])
    while not all(done):
        for s, gen in enumerate(streams):
            for _ in range(steps[s]):
                if not done[s]:
                    try:
                        next(gen)
                    except StopIteration:
                        done[s] = True


def _mixer_stream(x, wrefs, state_in, state_out, emit, *, rows, chunk, scan):
    (lbl_ref, nmw_ref, win_ref, hnw_ref, lvw_ref, lvb_ref, wmix_ref, bcol_ref,
     wa_ref, wb_ref, wo_ref) = wrefs
    fdim = HEADS * LANES
    gw = GROUPS * LANES
    d_model = x.shape[1]
    offs = [0]
    for width in (fdim, fdim, fdim, fdim, gw, gw, d_model, d_model):
        offs.append(offs[-1] + width)

    xn = _rms(x, nmw_ref[...]).astype(BF16)

    def proj(i):
        return _dot(xn, win_ref[:, offs[i]:offs[i + 1]])

    lbl = lbl_ref[...]
    lbe = jnp.exp(lbl - jnp.max(lbl, axis=0, keepdims=True))
    lb = lbe[0:1, :] / jnp.sum(lbe, axis=0, keepdims=True)

    fg = lb + (1.0 - lb) * jax.nn.sigmoid(proj(1))
    k = 1.0 - fg
    log_f = jnp.log(fg)
    lf_hi = log_f.astype(BF16)
    lf_rem = log_f - lf_hi.astype(F32)
    lf_mid = lf_rem.astype(BF16)
    lf_lo = (lf_rem - lf_mid.astype(F32)).astype(BF16)
    lf3 = jnp.concatenate([lf_hi, lf_mid, lf_lo], axis=1)
    yield
    q = jax.nn.silu(proj(0))
    yield
    v_in = proj(2)
    yield
    r128 = lax.broadcasted_iota(jnp.int32, (LANES, LANES), 0)
    c128 = lax.broadcasted_iota(jnp.int32, (LANES, LANES), 1)
    shift = min(chunk, LANES).bit_length() - 1
    seg = jnp.where(((r128 >> shift) == (c128 >> shift)) & (c128 <= r128), 1.0, 0.0).astype(BF16)
    b_rows = []
    for g in range(rows // LANES):
        b3 = _dot(seg, lf3[g * LANES:(g + 1) * LANES, :])
        b_rows.append(b3[:, :fdim] + b3[:, fdim:2 * fdim] + b3[:, 2 * fdim:])
    b = jnp.concatenate(b_rows, axis=0)
    yield

    def head(hd):
        cols = slice(hd * LANES, (hd + 1) * LANES)
        o, st_out = yield from _hgrn_head(q[:, cols], k[:, cols], b[:, cols], v_in[:, cols],
                                          state_in(hd), chunk, scan)
        state_out(hd, st_out)
        return o

    o_heads = [(yield from head(0))]
    yield
    og = jax.nn.sigmoid(proj(3))
    yield
    o_heads.append((yield from head(1)))
    yield
    u = _gelu(proj(4))
    yield
    o_heads.append((yield from head(2)))
    yield
    gv = _gelu(proj(5))
    gc = gv - jnp.mean(gv, axis=-1, keepdims=True)
    v = gc * lax.rsqrt(jnp.mean(gc * gc, axis=-1, keepdims=True) + EPS) * lvw_ref[...] + lvb_ref[...]
    emit("v", v)
    yield
    o_heads.append((yield from head(3)))
    yield
    gate_a = jax.nn.sigmoid(proj(6))
    yield
    gate_b = jax.nn.sigmoid(proj(7))
    yield
    o_a = _rms(jnp.concatenate(o_heads, axis=1) * og, hnw_ref[...]).astype(BF16)

    vb = v.astype(BF16)
    if scan:
        wm = [jnp.where(c128 <= r128, wmix_ref[g], 0.0).astype(BF16) for g in range(GROUPS)]
        bcol = [bcol_ref[g] for g in range(GROUPS)]
    else:
        reps = LANES // chunk
        spread = jnp.where((r128 < chunk) & ((c128 & (chunk - 1)) == r128), 1.0, 0.0).astype(BF16)
        inside = ((r128 >> shift) == (c128 >> shift)) & (c128 <= r128)
        wm, bcol = [], []
        for g in range(GROUPS):
            corner = _dot(wmix_ref[g, :chunk, :].astype(BF16), spread)
            wm.append(jnp.where(inside, jnp.concatenate([corner] * reps, axis=0), 0.0).astype(BF16))
            bcol.append(jnp.concatenate([bcol_ref[g, :chunk, :]] * reps, axis=0))
    n_rg = rows // LANES
    mixed_g = [
        _dot(wm[g], jnp.concatenate(
            [vb[rg * LANES:(rg + 1) * LANES, g * LANES:(g + 1) * LANES] for rg in range(n_rg)],
            axis=1)) for g in range(GROUPS)]
    a_out = _dot(o_a, wa_ref[...])
    yield
    mixed = jnp.concatenate(
        [jnp.concatenate([mixed_g[g][:, rg * LANES:(rg + 1) * LANES] + bcol[g]
                          for g in range(GROUPS)], axis=1) for rg in range(n_rg)], axis=0)
    o_b = (u * mixed).astype(BF16)
    yield

    mix = (gate_a * a_out + gate_b * _dot(o_b, wb_ref[...])).astype(BF16)
    yield
    emit("h", x + _dot(mix, wo_ref[...]))


def _ffn_stream(h, p, wrefs, prev_rows, keep_rows, emit, *, rows, tpos):
    (nfw_ref, wup_ref, cw_ref, cb_ref, wdn_ref, npw_ref, wpg_ref, wpp_ref, fnw_ref) = wrefs
    d_ff = wdn_ref.shape[0]
    n_tiles = d_ff // FF_TILE
    hn = _rms(h, nfw_ref[...]).astype(BF16)

    def conv(up, lo):
        cols = slice(lo, lo + FF_TILE)
        w = cw_ref[:, cols]
        e0, e1 = prev_rows(cols)
        keep_rows(cols, up)
        x1 = jnp.where(tpos >= 1, pltpu.roll(up, 1, 0), e1)
        x2 = jnp.where(tpos >= 2, pltpu.roll(up, 2, 0), jnp.where(tpos == 1, e1, e0))
        return cb_ref[:, cols] + x2 * w[0:1, :] + x1 * w[1:2, :] + up * w[2:3, :]

    def up_pair(j):
        lo_a = j * FF_TILE
        lo_b = d_ff + j * FF_TILE
        return (_dot(hn, wup_ref[:, lo_a:lo_a + FF_TILE]), _dot(hn, wup_ref[:, lo_b:lo_b + FF_TILE]))

    acc = jnp.zeros((rows, h.shape[1]), F32)
    nxt = up_pair(0)
    pp = _dot(p.astype(BF16), wpp_ref[...])
    gated = None
    yield
    for j in range(n_tiles + 1):
        lo_a = j * FF_TILE
        cur = nxt
        if j + 1 < n_tiles:
            nxt = up_pair(j + 1)
        if j < n_tiles:
            ya = conv(cur[0], lo_a)
            yb = conv(cur[1], d_ff + lo_a)
            new_gated = (_gelu(ya) * yb).astype(BF16)
        if j > 0:
            acc = acc + _dot(gated, wdn_ref[lo_a - FF_TILE:lo_a, :])
        gated = new_gated
        yield
    h2 = h + acc
    gate = jax.nn.sigmoid(_dot(_rms(h2, npw_ref[...]).astype(BF16), wpg_ref[...]))
    yield
    h3 = h2 + gate * pp
    emit("y", _rms(h3, fnw_ref[...]))


def _prompt_kernel(*refs, rows, chunk, n_t, n_blocks):
    x_ref, p_ref = refs[:2]
    mixer_w = refs[2:13]
    ffn_w = refs[13:22]
    y_ref, sout_ref, cout_ref, st_s, carry_s, h_s = refs[22:]
    i = pl.program_id(0)
    t_mix = lax.rem(i, n_t)
    t_ffn = lax.rem(jnp.maximum(i - 1, 0), n_t)

    @pl.when(t_mix == 0)
    def _():
        st_s[...] = jnp.zeros_like(st_s)

    @pl.when(t_ffn == 0)
    def _():
        carry_s[...] = jnp.zeros_like(carry_s)

    slot = lax.rem(i, 2)

    def state_out(hd, states):
        st_s[hd] = states[-1]

    def prev_rows(cols):
        return carry_s[0:1, cols], carry_s[1:2, cols]

    def keep_rows(cols, up):
        carry_s[:, cols] = up[rows - 2:rows, :]
        cout_ref[0, :, cols] = up[rows - 2:rows, :]

    tpos = lax.broadcasted_iota(jnp.int32, (rows, FF_TILE), 0)

    def step(mixer, ffn):
        results = {}
        streams = []
        if mixer:
            streams.append(_mixer_stream(x_ref[...], mixer_w, lambda hd: st_s[hd], state_out,
                                         results.__setitem__, rows=rows, chunk=chunk, scan=True))
        if ffn:
            streams.append(_ffn_stream(h_s[1 - slot], p_ref[...], ffn_w, prev_rows, keep_rows,
                                       results.__setitem__, rows=rows, tpos=tpos))
        _interleave(streams, steps=PHASES if mixer and ffn else (1,), lead=1)
        if ffn:
            y_ref[...] = results["y"]
        if mixer:
            h_s[slot] = results["h"]

    pl.when(i == 0)(functools.partial(step, True, False))
    pl.when((i > 0) & (i < n_blocks))(functools.partial(step, True, True))
    pl.when(i == n_blocks)(functools.partial(step, False, True))

    @pl.when((t_mix == n_t - 1) & (i < n_blocks))
    def _():
        for hd in range(HEADS):
            sout_ref[0, hd] = st_s[hd].T


def _mixer_sample_kernel(*refs, rows, seq):
    x_ref = refs[0]
    wrefs = refs[1:12]
    sin_ref, h_ref, sout_ref, vout_ref = refs[12:]
    n_seq = rows // seq
    results = {}

    def state_out(hd, states):
        for j in range(n_seq):
            sout_ref[j, hd] = states[j].T

    for _ in _mixer_stream(x_ref[...], wrefs,
                           lambda hd: [sin_ref[j, hd].T for j in range(n_seq)], state_out,
                           results.__setitem__, rows=rows, chunk=seq, scan=False):
        pass
    vout_ref[...] = results["v"]
    h_ref[...] = results["h"]


def _ffn_sample_kernel(*refs, rows, seq):
    h_ref, p_ref = refs[:2]
    wrefs = refs[2:11]
    cst_ref, y_ref, cout_ref = refs[11:]
    n_seq = rows // seq
    results = {}

    def prev_rows(cols):
        st = cst_ref[:, :, cols]
        width = st.shape[-1]
        return tuple(jnp.broadcast_to(st[:, r:r + 1, :], (n_seq, seq, width)).reshape(rows, width)
                     for r in range(CONV_W - 1))

    def keep_rows(cols, up):
        cout_ref[:, :, cols] = up.reshape(n_seq, seq, up.shape[-1])[:, seq - (CONV_W - 1):seq, :]

    tpos = lax.broadcasted_iota(jnp.int32, (rows, FF_TILE), 0) & (seq - 1)
    for _ in _ffn_stream(h_ref[...], p_ref[...], wrefs, prev_rows, keep_rows, results.__setitem__,
                         rows=rows, tpos=tpos):
        pass
    y_ref[...] = results["y"]


def _const_spec(shape):
    nd = len(shape)
    return pl.BlockSpec(shape, lambda *_: (0,) * nd, pipeline_mode=pl.Buffered(1))


def _params(n_axes):
    return pltpu.CompilerParams(dimension_semantics=("arbitrary",) * n_axes,
                                vmem_limit_bytes=VMEM_LIMIT)


def _cast_kernel(*refs):
    n = len(refs) // 2
    for src, dst in zip(refs[:n], refs[n:]):
        dst[...] = src[...].astype(dst.dtype)


def _to_bf16(mats):
    tile_rows = 2 * SUBLANES
    for m in mats:
        assert m.shape[0] % (CAST_STEPS * tile_rows) == 0 and m.shape[1] % LANES == 0
    specs = [pl.BlockSpec((m.shape[0] // CAST_STEPS, m.shape[1]), lambda i: (i, 0)) for m in mats]
    return pl.pallas_call(
        _cast_kernel, grid=(CAST_STEPS,), in_specs=specs, out_specs=specs,
        out_shape=[jax.ShapeDtypeStruct(m.shape, BF16) for m in mats],
        compiler_params=_params(1), name="weights_to_bf16",
    )(*mats)


def _prompt(x, p, mixer_consts, ffn_consts, *, rows, chunk):
    batch, seq, d = x.shape
    n_t = seq // rows
    n_blocks = batch * n_t
    width = ffn_consts[2].shape[1]
    mix_blk = lambda i: jnp.minimum(i, n_blocks - 1)
    ffn_blk = lambda i: jnp.maximum(i - 1, 0)
    consts = tuple(mixer_consts) + tuple(ffn_consts)
    in_specs = ([pl.BlockSpec((rows, d), lambda i: (mix_blk(i), 0)),
                 pl.BlockSpec((rows, p.shape[-1]), lambda i: (ffn_blk(i), 0))]
                + [_const_spec(a.shape) for a in consts])
    out_specs = [pl.BlockSpec((rows, d), lambda i: (ffn_blk(i), 0)),
                 pl.BlockSpec((1, HEADS, LANES, LANES), lambda i: (mix_blk(i) // n_t, 0, 0, 0)),
                 pl.BlockSpec((1, CONV_W - 1, width), lambda i: (ffn_blk(i) // n_t, 0, 0))]
    out_shape = [jax.ShapeDtypeStruct((batch * seq, d), F32),
                 jax.ShapeDtypeStruct((batch, HEADS, LANES, LANES), F32),
                 jax.ShapeDtypeStruct((batch, CONV_W - 1, width), F32)]
    scratch = [pltpu.VMEM((HEADS, LANES, LANES), F32), pltpu.VMEM((CONV_W - 1, width), F32),
               pltpu.VMEM((2, rows, d), F32)]
    y, st, cs = pl.pallas_call(
        functools.partial(_prompt_kernel, rows=rows, chunk=chunk, n_t=n_t, n_blocks=n_blocks),
        grid=(n_blocks + 1,), in_specs=in_specs, out_specs=out_specs, out_shape=out_shape,
        scratch_shapes=scratch, compiler_params=_params(1), name="prompt_step",
    )(x.reshape(batch * seq, d), p.reshape(batch * seq, -1), *consts)
    return y.reshape(batch, seq, d), st, cs


def _sample(x, p, state, conv_state, mixer_consts, ffn_consts, *, mixer_rows, ffn_rows):
    batch, seq, d = x.shape
    width = ffn_consts[2].shape[1]
    gw = GROUPS * LANES
    rows = mixer_rows
    n_seq = rows // seq
    row_map = lambda i: (i, 0)
    st_spec = pl.BlockSpec((n_seq, HEADS, LANES, LANES), lambda i: (i, 0, 0, 0))
    h, st, v = pl.pallas_call(
        functools.partial(_mixer_sample_kernel, rows=rows, seq=seq),
        grid=(batch // n_seq,),
        in_specs=([pl.BlockSpec((rows, d), row_map)] + [_const_spec(a.shape) for a in mixer_consts]
                  + [st_spec]),
        out_specs=[pl.BlockSpec((rows, d), row_map), st_spec, pl.BlockSpec((rows, gw), row_map)],
        out_shape=[jax.ShapeDtypeStruct((batch * seq, d), F32),
                   jax.ShapeDtypeStruct(state.shape, F32),
                   jax.ShapeDtypeStruct((batch * seq, gw), F32)],
        compiler_params=_params(1), name="mixer_sample",
    )(x.reshape(batch * seq, d), *mixer_consts, state)
    rows = ffn_rows
    n_seq = rows // seq
    cs_spec = pl.BlockSpec((n_seq, CONV_W - 1, width), lambda i: (i, 0, 0))
    y, cs = pl.pallas_call(
        functools.partial(_ffn_sample_kernel, rows=rows, seq=seq),
        grid=(batch // n_seq,),
        in_specs=([pl.BlockSpec((rows, d), row_map), pl.BlockSpec((rows, p.shape[-1]), row_map)]
                  + [_const_spec(a.shape) for a in ffn_consts] + [cs_spec]),
        out_specs=[pl.BlockSpec((rows, d), row_map), cs_spec],
        out_shape=[jax.ShapeDtypeStruct((batch * seq, d), F32),
                   jax.ShapeDtypeStruct(conv_state.shape, F32)],
        compiler_params=_params(1), name="ffn_sample",
    )(h, p.reshape(batch * seq, -1), *ffn_consts, conv_state)
    return y.reshape(batch, seq, d), st, cs, v.reshape(batch, seq, gw)


def kernel(x_prompt, x_sample, p_prompt, p_sample, state_hgrn, state_conv, lb_logits, norm_mix_w, w_in, hgrn_norm_w, ln_v_w, ln_v_b, w_spatial, b_spatial, w_a_out, w_b_out, w_o, norm_ffn_w, w_up, conv_w, conv_b, w_down, norm_ple_w, w_ple_gate, w_ple_proj, final_norm_w):
    depth = w_in.shape[0]
    assert depth == 1 and lb_logits.shape[0] == 2
    seq_s = x_sample.shape[1]
    assert seq_s == SUBLANES and LANES % seq_s == 0
    l = 0
    row = lambda a: a.reshape(1, -1)


    (w_in_b, w_a_b, w_b_b, w_o_b, w_up_b, w_down_b, w_pg_b, w_pp_b) = _to_bf16(
        [w_in[l], w_a_out[l], w_b_out[l], w_o[l], w_up[l], w_down[l], w_ple_gate[l], w_ple_proj[l]])

    mixer_consts = (lb_logits, row(norm_mix_w[l]), w_in_b, row(hgrn_norm_w[l]), row(ln_v_w[l]),
                    row(ln_v_b[l]), w_spatial[l], b_spatial[l][:, :, None], w_a_b, w_b_b, w_o_b)

    ffn_consts = (row(norm_ffn_w[l]), w_up_b, conv_w[l], row(conv_b[l]), w_down_b,
                  row(norm_ple_w[l]), w_pg_b, w_pp_b, row(final_norm_w))

    y_p, hs_p, cs_p = _prompt(x_prompt, p_prompt[l], mixer_consts, ffn_consts,
                              rows=BLOCK_ROWS, chunk=HGRN_CHUNK)
    y_s, hs_s, cs_s, v_s = _sample(x_sample, p_sample[l], state_hgrn[l], state_conv[l],
                                   mixer_consts, ffn_consts,
                                   mixer_rows=LANES, ffn_rows=BLOCK_ROWS)
    return (y_p, y_s, hs_p[None], hs_s[None], cs_p[None], cs_s[None], v_s[None])
```

```python
import functools

import jax
import jax.numpy as jnp
from jax import lax
from jax.experimental import pallas as pl
from jax.experimental.pallas import tpu as pltpu

F32 = jnp.float32
BF16 = jnp.bfloat16
EPS = 1e-6

HEADS = 4
GROUPS = 4
CONV_W = 3
LANES = 128
SUBLANES = 8
FF_TILE = 256
CAST_STEPS = 8
BLOCK_ROWS = 256
HGRN_CHUNK = 64
PHASES = (2, 1)
VMEM_LIMIT = 56 * 1024 * 1024

NT_DIMS = (((1,), (1,)), ((), ()))
TN_DIMS = (((0,), (0,)), ((), ()))


def _rms(x, w):
    return x * lax.rsqrt(jnp.mean(x * x, axis=-1, keepdims=True) + EPS) * w


def _gelu(x):
    return 0.5 * x * (1.0 + lax.erf(x * (2.0 ** -0.5)))


def _dot(a, b):
    return jnp.dot(a, b, preferred_element_type=F32)


def _hgrn_head(q, k, b, v, st_in, chunk, scan):
    rows = q.shape[0]
    n = rows // chunk
    shift = chunk.bit_length() - 1
    b3 = b.reshape(n, chunk, LANES)

    def chunk_row(r):
        return jnp.broadcast_to(b3[:, r:r + 1, :], b3.shape).reshape(rows, LANES)

    def blocked(a):
        cols = []
        for j in range(n):
            parts = [jnp.zeros((j * chunk, LANES), F32), a[j * chunk:(j + 1) * chunk],
                     jnp.zeros((rows - (j + 1) * chunk, LANES), F32)]
            cols.append(jnp.concatenate([p for p in parts if p.shape[0]], axis=0))
        return jnp.concatenate(cols, axis=1).astype(BF16)

    b_mid = chunk_row(chunk // 2 - 1)
    b_end = chunk_row(chunk - 1)
    q_t = (q * jnp.exp(b - b_mid)).astype(BF16)
    k_t = (k * jnp.exp(b_mid - b)).astype(BF16)
    vb = v.astype(BF16)
    k_blk = blocked(k * jnp.exp(b_end - b))
    q_blk = blocked(q * jnp.exp(b))
    yield
    scores = lax.dot_general(q_t, k_t, NT_DIMS, preferred_element_type=F32)
    ds = lax.dot_general(vb, k_blk, TN_DIMS, preferred_element_type=F32)
    yield
    row = lax.broadcasted_iota(jnp.int32, (rows, rows), 0)
    col = lax.broadcasted_iota(jnp.int32, (rows, rows), 1)
    keep = ((row >> shift) == (col >> shift)) & (col <= row)
    scores = jnp.where(keep, scores, 0.0).astype(BF16)
    before, after = [], []
    st = st_in
    for j in range(n):
        st_j = st if scan else st_in[j]
        before.append(st_j.astype(BF16))
        st = st_j * jnp.exp(b3[j, chunk - 1:chunk, :]) + ds[:, j * LANES:(j + 1) * LANES]
        after.append(st)
    s_rhs = jnp.concatenate(before, axis=1)
    yield
    o = _dot(scores, vb) + lax.dot_general(q_blk, s_rhs, NT_DIMS, preferred_element_type=F32)
    return o, after


def _interleave(streams, steps, lead=0):
    done = [False] * len(streams)
    for _ in range(lead):
        next(streams[0])
    while not all(done):
        for s, gen in enumerate(streams):
            for _ in range(steps[s]):
                if not done[s]:
                    try:
                        next(gen)
                    except StopIteration:
                        done[s] = True


def _mixer_stream(x, wrefs, state_in, state_out, emit, *, rows, chunk, scan):
    (lbl_ref, nmw_ref, win_ref, hnw_ref, lvw_ref, lvb_ref, wmix_ref, bcol_ref,
     wa_ref, wb_ref, wo_ref) = wrefs
    fdim = HEADS * LANES
    gw = GROUPS * LANES
    d_model = x.shape[1]
    offs = [0]
    for width in (fdim, fdim, fdim, fdim, gw, gw, d_model, d_model):
        offs.append(offs[-1] + width)

    xn = _rms(x, nmw_ref[...]).astype(BF16)

    def proj(i):
        return _dot(xn, win_ref[:, offs[i]:offs[i + 1]])

    lbl = lbl_ref[...]
    lbe = jnp.exp(lbl - jnp.max(lbl, axis=0, keepdims=True))
    lb = lbe[0:1, :] / jnp.sum(lbe, axis=0, keepdims=True)

    fg = lb + (1.0 - lb) * jax.nn.sigmoid(proj(1))
    k = 1.0 - fg
    log_f = jnp.log(fg)
    lf_hi = log_f.astype(BF16)
    lf_rem = log_f - lf_hi.astype(F32)
    lf_mid = lf_rem.astype(BF16)
    lf_lo = (lf_rem - lf_mid.astype(F32)).astype(BF16)
    lf3 = jnp.concatenate([lf_hi, lf_mid, lf_lo], axis=1)
    yield
    q = jax.nn.silu(proj(0))
    yield
    v_in = proj(2)
    yield
    r128 = lax.broadcasted_iota(jnp.int32, (LANES, LANES), 0)
    c128 = lax.broadcasted_iota(jnp.int32, (LANES, LANES), 1)
    shift = min(chunk, LANES).bit_length() - 1
    seg = jnp.where(((r128 >> shift) == (c128 >> shift)) & (c128 <= r128), 1.0, 0.0).astype(BF16)
    b_rows = []
    for g in range(rows // LANES):
        b3 = _dot(seg, lf3[g * LANES:(g + 1) * LANES, :])
        b_rows.append(b3[:, :fdim] + b3[:, fdim:2 * fdim] + b3[:, 2 * fdim:])
    b = jnp.concatenate(b_rows, axis=0)
    yield

    def head(hd):
        cols = slice(hd * LANES, (hd + 1) * LANES)
        o, st_out = yield from _hgrn_head(q[:, cols], k[:, cols], b[:, cols], v_in[:, cols],
                                          state_in(hd), chunk, scan)
        state_out(hd, st_out)
        return o

    o_heads = [(yield from head(0))]
    yield
    og = jax.nn.sigmoid(proj(3))
    yield
    o_heads.append((yield from head(1)))
    yield
    u = _gelu(proj(4))
    yield
    o_heads.append((yield from head(2)))
    yield
    gv = _gelu(proj(5))
    gc = gv - jnp.mean(gv, axis=-1, keepdims=True)
    v = gc * lax.rsqrt(jnp.mean(gc * gc, axis=-1, keepdims=True) + EPS) * lvw_ref[...] + lvb_ref[...]
    emit("v", v)
    yield
    o_heads.append((yield from head(3)))
    yield
    gate_a = jax.nn.sigmoid(proj(6))
    yield
    gate_b = jax.nn.sigmoid(proj(7))
    yield
    o_a = _rms(jnp.concatenate(o_heads, axis=1) * og, hnw_ref[...]).astype(BF16)

    vb = v.astype(BF16)
    if scan:
        wm = [jnp.where(c128 <= r128, wmix_ref[g], 0.0).astype(BF16) for g in range(GROUPS)]
        bcol = [bcol_ref[g] for g in range(GROUPS)]
    else:
        reps = LANES // chunk
        spread = jnp.where((r128 < chunk) & ((c128 & (chunk - 1)) == r128), 1.0, 0.0).astype(BF16)
        inside = ((r128 >> shift) == (c128 >> shift)) & (c128 <= r128)
        wm, bcol = [], []
        for g in range(GROUPS):
            corner = _dot(wmix_ref[g, :chunk, :].astype(BF16), spread)
            wm.append(jnp.where(inside, jnp.concatenate([corner] * reps, axis=0), 0.0).astype(BF16))
            bcol.append(jnp.concatenate([bcol_ref[g, :chunk, :]] * reps, axis=0))
    n_rg = rows // LANES
    mixed_g = [
        _dot(wm[g], jnp.concatenate(
            [vb[rg * LANES:(rg + 1) * LANES, g * LANES:(g + 1) * LANES] for rg in range(n_rg)],
            axis=1)) for g in range(GROUPS)]
    a_out = _dot(o_a, wa_ref[...])
    yield
    mixed = jnp.concatenate(
        [jnp.concatenate([mixed_g[g][:, rg * LANES:(rg + 1) * LANES] + bcol[g]
                          for g in range(GROUPS)], axis=1) for rg in range(n_rg)], axis=0)
    o_b = (u * mixed).astype(BF16)
    yield

    mix = (gate_a * a_out + gate_b * _dot(o_b, wb_ref[...])).astype(BF16)
    yield
    emit("h", x + _dot(mix, wo_ref[...]))


def _ffn_stream(h, p, wrefs, prev_rows, keep_rows, emit, *, rows, tpos, period=None):
    (nfw_ref, wup_ref, cw_ref, cb_ref, wdn_ref, npw_ref, wpg_ref, wpp_ref, fnw_ref) = wrefs
    d_ff = wdn_ref.shape[0]
    n_tiles = d_ff // FF_TILE
    hn = _rms(h, nfw_ref[...]).astype(BF16)

    def shift(a, k):
        if period is None:
            return pltpu.roll(a, k, 0)
        grouped = a.reshape(rows // period, period, a.shape[-1])
        return pltpu.roll(grouped, k, 1).reshape(a.shape)

    def conv(up, lo):
        cols = slice(lo, lo + FF_TILE)
        w = cw_ref[:, cols]
        e0, e1 = prev_rows(cols)
        keep_rows(cols, up)
        x1 = jnp.where(tpos >= 1, shift(up, 1), e1)
        x2 = jnp.where(tpos >= 2, shift(up, 2), jnp.where(tpos == 1, e1, e0))
        return cb_ref[:, cols] + x2 * w[0:1, :] + x1 * w[1:2, :] + up * w[2:3, :]

    def up_pair(j):
        lo_a = j * FF_TILE
        lo_b = d_ff + j * FF_TILE
        return (_dot(hn, wup_ref[:, lo_a:lo_a + FF_TILE]), _dot(hn, wup_ref[:, lo_b:lo_b + FF_TILE]))

    acc = jnp.zeros((rows, h.shape[1]), F32)
    nxt = up_pair(0)
    pp = _dot(p.astype(BF16), wpp_ref[...])
    gated = None
    yield
    for j in range(n_tiles + 1):
        lo_a = j * FF_TILE
        cur = nxt
        if j + 1 < n_tiles:
            nxt = up_pair(j + 1)
        if j < n_tiles:
            ya = conv(cur[0], lo_a)
            yb = conv(cur[1], d_ff + lo_a)
            new_gated = (_gelu(ya) * yb).astype(BF16)
        if j > 0:
            acc = acc + _dot(gated, wdn_ref[lo_a - FF_TILE:lo_a, :])
        gated = new_gated
        yield
    h2 = h + acc
    gate = jax.nn.sigmoid(_dot(_rms(h2, npw_ref[...]).astype(BF16), wpg_ref[...]))
    yield
    h3 = h2 + gate * pp
    emit("y", _rms(h3, fnw_ref[...]))


def _prompt_kernel(*refs, rows, chunk, n_t, n_blocks):
    x_ref, p_ref = refs[:2]
    mixer_w = refs[2:13]
    ffn_w = refs[13:22]
    y_ref, sout_ref, cout_ref, st_s, carry_s, h_s = refs[22:]
    i = pl.program_id(0)
    t_mix = lax.rem(i, n_t)
    t_ffn = lax.rem(jnp.maximum(i - 1, 0), n_t)

    @pl.when(t_mix == 0)
    def _():
        st_s[...] = jnp.zeros_like(st_s)

    @pl.when(t_ffn == 0)
    def _():
        carry_s[...] = jnp.zeros_like(carry_s)

    slot = lax.rem(i, 2)

    def state_out(hd, states):
        st_s[hd] = states[-1]

    def prev_rows(cols):
        return carry_s[0:1, cols], carry_s[1:2, cols]

    def keep_rows(cols, up):
        carry_s[:, cols] = up[rows - 2:rows, :]
        cout_ref[0, :, cols] = up[rows - 2:rows, :]

    tpos = lax.broadcasted_iota(jnp.int32, (rows, FF_TILE), 0)

    def step(mixer, ffn):
        results = {}
        streams = []
        if mixer:
            streams.append(_mixer_stream(x_ref[...], mixer_w, lambda hd: st_s[hd], state_out,
                                         results.__setitem__, rows=rows, chunk=chunk, scan=True))
        if ffn:
            streams.append(_ffn_stream(h_s[1 - slot], p_ref[...], ffn_w, prev_rows, keep_rows,
                                       results.__setitem__, rows=rows, tpos=tpos))
        _interleave(streams, steps=PHASES if mixer and ffn else (1,), lead=1)
        if ffn:
            y_ref[...] = results["y"]
        if mixer:
            h_s[slot] = results["h"]

    pl.when(i == 0)(functools.partial(step, True, False))
    pl.when((i > 0) & (i < n_blocks))(functools.partial(step, True, True))
    pl.when(i == n_blocks)(functools.partial(step, False, True))

    @pl.when((t_mix == n_t - 1) & (i < n_blocks))
    def _():
        for hd in range(HEADS):
            sout_ref[0, hd] = st_s[hd].T


def _mixer_sample_kernel(*refs, rows, seq):
    x_ref = refs[0]
    wrefs = refs[1:12]
    sin_ref, h_ref, sout_ref, vout_ref = refs[12:]
    n_seq = rows // seq
    results = {}

    def state_out(hd, states):
        for j in range(n_seq):
            sout_ref[j, hd] = states[j].T

    for _ in _mixer_stream(x_ref[...], wrefs,
                           lambda hd: [sin_ref[j, hd].T for j in range(n_seq)], state_out,
                           results.__setitem__, rows=rows, chunk=seq, scan=False):
        pass
    vout_ref[...] = results["v"]
    h_ref[...] = results["h"]


def _ffn_sample_kernel(*refs, rows, seq):
    h_ref, p_ref = refs[:2]
    wrefs = refs[2:11]
    cst_ref, y_ref, cout_ref = refs[11:]
    n_seq = rows // seq
    results = {}

    def prev_rows(cols):
        st = cst_ref[:, :, cols]
        width = st.shape[-1]
        return tuple(jnp.broadcast_to(st[:, r:r + 1, :], (n_seq, seq, width)).reshape(rows, width)
                     for r in range(CONV_W - 1))

    def keep_rows(cols, up):
        cout_ref[:, :, cols] = up.reshape(n_seq, seq, up.shape[-1])[:, seq - (CONV_W - 1):seq, :]

    tpos = lax.broadcasted_iota(jnp.int32, (rows, FF_TILE), 0) & (seq - 1)
    for _ in _ffn_stream(h_ref[...], p_ref[...], wrefs, prev_rows, keep_rows, results.__setitem__,
                         rows=rows, tpos=tpos, period=seq):
        pass
    y_ref[...] = results["y"]


def _const_spec(shape):
    nd = len(shape)
    return pl.BlockSpec(shape, lambda *_: (0,) * nd, pipeline_mode=pl.Buffered(1))


def _params(n_axes):
    return pltpu.CompilerParams(dimension_semantics=("arbitrary",) * n_axes,
                                vmem_limit_bytes=VMEM_LIMIT)


def _cast_kernel(*refs):
    n = len(refs) // 2
    for src, dst in zip(refs[:n], refs[n:]):
        dst[...] = src[...].astype(dst.dtype)


def _to_bf16(mats):
    tile_rows = 2 * SUBLANES
    for m in mats:
        assert m.shape[0] % (CAST_STEPS * tile_rows) == 0 and m.shape[1] % LANES == 0
    specs = [pl.BlockSpec((m.shape[0] // CAST_STEPS, m.shape[1]), lambda i: (i, 0)) for m in mats]
    return pl.pallas_call(
        _cast_kernel, grid=(CAST_STEPS,), in_specs=specs, out_specs=specs,
        out_shape=[jax.ShapeDtypeStruct(m.shape, BF16) for m in mats],
        compiler_params=_params(1), name="weights_to_bf16",
    )(*mats)


def _prompt(x, p, mixer_consts, ffn_consts, *, rows, chunk):
    batch, seq, d = x.shape
    n_t = seq // rows
    n_blocks = batch * n_t
    width = ffn_consts[2].shape[1]
    mix_blk = lambda i: jnp.minimum(i, n_blocks - 1)
    ffn_blk = lambda i: jnp.maximum(i - 1, 0)
    consts = tuple(mixer_consts) + tuple(ffn_consts)
    in_specs = ([pl.BlockSpec((rows, d), lambda i: (mix_blk(i), 0)),
                 pl.BlockSpec((rows, p.shape[-1]), lambda i: (ffn_blk(i), 0))]
                + [_const_spec(a.shape) for a in consts])
    out_specs = [pl.BlockSpec((rows, d), lambda i: (ffn_blk(i), 0)),
                 pl.BlockSpec((1, HEADS, LANES, LANES), lambda i: (mix_blk(i) // n_t, 0, 0, 0)),
                 pl.BlockSpec((1, CONV_W - 1, width), lambda i: (ffn_blk(i) // n_t, 0, 0))]
    out_shape = [jax.ShapeDtypeStruct((batch * seq, d), F32),
                 jax.ShapeDtypeStruct((batch, HEADS, LANES, LANES), F32),
                 jax.ShapeDtypeStruct((batch, CONV_W - 1, width), F32)]
    scratch = [pltpu.VMEM((HEADS, LANES, LANES), F32), pltpu.VMEM((CONV_W - 1, width), F32),
               pltpu.VMEM((2, rows, d), F32)]
    y, st, cs = pl.pallas_call(
        functools.partial(_prompt_kernel, rows=rows, chunk=chunk, n_t=n_t, n_blocks=n_blocks),
        grid=(n_blocks + 1,), in_specs=in_specs, out_specs=out_specs, out_shape=out_shape,
        scratch_shapes=scratch, compiler_params=_params(1), name="prompt_step",
    )(x.reshape(batch * seq, d), p.reshape(batch * seq, -1), *consts)
    return y.reshape(batch, seq, d), st, cs


def _sample(x, p, state, conv_state, mixer_consts, ffn_consts, *, mixer_rows, ffn_rows):
    batch, seq, d = x.shape
    width = ffn_consts[2].shape[1]
    gw = GROUPS * LANES
    rows = mixer_rows
    n_seq = rows // seq
    row_map = lambda i: (i, 0)
    st_spec = pl.BlockSpec((n_seq, HEADS, LANES, LANES), lambda i: (i, 0, 0, 0))
    h, st, v = pl.pallas_call(
        functools.partial(_mixer_sample_kernel, rows=rows, seq=seq),
        grid=(batch // n_seq,),
        in_specs=([pl.BlockSpec((rows, d), row_map)] + [_const_spec(a.shape) for a in mixer_consts]
                  + [st_spec]),
        out_specs=[pl.BlockSpec((rows, d), row_map), st_spec, pl.BlockSpec((rows, gw), row_map)],
        out_shape=[jax.ShapeDtypeStruct((batch * seq, d), F32),
                   jax.ShapeDtypeStruct(state.shape, F32),
                   jax.ShapeDtypeStruct((batch * seq, gw), F32)],
        compiler_params=_params(1), name="mixer_sample",
    )(x.reshape(batch * seq, d), *mixer_consts, state)
    rows = ffn_rows
    n_seq = rows // seq
    cs_spec = pl.BlockSpec((n_seq, CONV_W - 1, width), lambda i: (i, 0, 0))
    y, cs = pl.pallas_call(
        functools.partial(_ffn_sample_kernel, rows=rows, seq=seq),
        grid=(batch // n_seq,),
        in_specs=([pl.BlockSpec((rows, d), row_map), pl.BlockSpec((rows, p.shape[-1]), row_map)]
                  + [_const_spec(a.shape) for a in ffn_consts] + [cs_spec]),
        out_specs=[pl.BlockSpec((rows, d), row_map), cs_spec],
        out_shape=[jax.ShapeDtypeStruct((batch * seq, d), F32),
                   jax.ShapeDtypeStruct(conv_state.shape, F32)],
        compiler_params=_params(1), name="ffn_sample",
    )(h, p.reshape(batch * seq, -1), *ffn_consts, conv_state)
    return y.reshape(batch, seq, d), st, cs, v.reshape(batch, seq, gw)


def kernel(x_prompt, x_sample, p_prompt, p_sample, state_hgrn, state_conv, lb_logits, norm_mix_w, w_in, hgrn_norm_w, ln_v_w, ln_v_b, w_spatial, b_spatial, w_a_out, w_b_out, w_o, norm_ffn_w, w_up, conv_w, conv_b, w_down, norm_ple_w, w_ple_gate, w_ple_proj, final_norm_w):
    depth = w_in.shape[0]
    assert depth == 1 and lb_logits.shape[0] == 2
    seq_s = x_sample.shape[1]
    assert seq_s == SUBLANES and LANES % seq_s == 0
    l = 0
    row = lambda a: a.reshape(1, -1)


    (w_in_b, w_a_b, w_b_b, w_o_b, w_up_b, w_down_b, w_pg_b, w_pp_b) = _to_bf16(
        [w_in[l], w_a_out[l], w_b_out[l], w_o[l], w_up[l], w_down[l], w_ple_gate[l], w_ple_proj[l]])

    mixer_consts = (lb_logits, row(norm_mix_w[l]), w_in_b, row(hgrn_norm_w[l]), row(ln_v_w[l]),
                    row(ln_v_b[l]), w_spatial[l], b_spatial[l][:, :, None], w_a_b, w_b_b, w_o_b)

    ffn_consts = (row(norm_ffn_w[l]), w_up_b, conv_w[l], row(conv_b[l]), w_down_b,
                  row(norm_ple_w[l]), w_pg_b, w_pp_b, row(final_norm_w))

    y_p, hs_p, cs_p = _prompt(x_prompt, p_prompt[l], mixer_consts, ffn_consts,
                              rows=BLOCK_ROWS, chunk=HGRN_CHUNK)
    y_s, hs_s, cs_s, v_s = _sample(x_sample, p_sample[l], state_hgrn[l], state_conv[l],
                                   mixer_consts, ffn_consts,
                                   mixer_rows=LANES, ffn_rows=BLOCK_ROWS)
    return (y_p, y_s, hs_p[None], hs_s[None], cs_p[None], cs_s[None], v_s[None])
```

```python
import functools

import jax
import jax.numpy as jnp
from jax import lax
from jax.experimental import pallas as pl
from jax.experimental.pallas import tpu as pltpu

F32 = jnp.float32
BF16 = jnp.bfloat16
EPS = 1e-6

HEADS = 4
GROUPS = 4
CONV_W = 3
LANES = 128
SUBLANES = 8
FF_TILE = 256
CAST_STEPS = 8
BLOCK_ROWS = 256
HGRN_CHUNK = 64
PHASES = (2, 1)
VMEM_LIMIT = 56 * 1024 * 1024

NT_DIMS = (((1,), (1,)), ((), ()))
TN_DIMS = (((0,), (0,)), ((), ()))


def _rms(x, w):
    return x * lax.rsqrt(jnp.mean(x * x, axis=-1, keepdims=True) + EPS) * w


def _gelu(x):
    return 0.5 * x * (1.0 + lax.erf(x * (2.0 ** -0.5)))


def _dot(a, b):
    return jnp.dot(a, b, preferred_element_type=F32)


def _hgrn_head(q, k, b, v, st_in, chunk, scan):
    rows = q.shape[0]
    n = rows // chunk
    shift = chunk.bit_length() - 1
    b3 = b.reshape(n, chunk, LANES)

    def chunk_row(r):
        return jnp.broadcast_to(b3[:, r:r + 1, :], b3.shape).reshape(rows, LANES)

    def blocked(a):
        cols = []
        for j in range(n):
            parts = [jnp.zeros((j * chunk, LANES), F32), a[j * chunk:(j + 1) * chunk],
                     jnp.zeros((rows - (j + 1) * chunk, LANES), F32)]
            cols.append(jnp.concatenate([p for p in parts if p.shape[0]], axis=0))
        return jnp.concatenate(cols, axis=1).astype(BF16)

    b_mid = chunk_row(chunk // 2 - 1)
    b_end = chunk_row(chunk - 1)
    q_t = (q * jnp.exp(b - b_mid)).astype(BF16)
    k_t = (k * jnp.exp(b_mid - b)).astype(BF16)
    vb = v.astype(BF16)
    k_blk = blocked(k * jnp.exp(b_end - b))
    q_blk = blocked(q * jnp.exp(b))
    yield
    scores = lax.dot_general(q_t, k_t, NT_DIMS, preferred_element_type=F32)
    ds = lax.dot_general(vb, k_blk, TN_DIMS, preferred_element_type=F32)
    yield
    row = lax.broadcasted_iota(jnp.int32, (rows, rows), 0)
    col = lax.broadcasted_iota(jnp.int32, (rows, rows), 1)
    keep = ((row >> shift) == (col >> shift)) & (col <= row)
    scores = jnp.where(keep, scores, 0.0).astype(BF16)
    before, after = [], []
    st = st_in
    for j in range(n):
        st_j = st if scan else st_in[j]
        before.append(st_j.astype(BF16))
        st = st_j * jnp.exp(b3[j, chunk - 1:chunk, :]) + ds[:, j * LANES:(j + 1) * LANES]
        after.append(st)
    s_rhs = jnp.concatenate(before, axis=1)
    yield
    o = _dot(scores, vb) + lax.dot_general(q_blk, s_rhs, NT_DIMS, preferred_element_type=F32)
    return o, after


def _interleave(streams, steps, lead=0):
    done = [False] * len(streams)
    for _ in range(lead):
        next(streams[0])
    while not all(done):
        for s, gen in enumerate(streams):
            for _ in range(steps[s]):
                if not done[s]:
                    try:
                        next(gen)
                    except StopIteration:
                        done[s] = True


def _mixer_stream(x, wrefs, state_in, state_out, emit, *, rows, chunk, scan):
    (lbl_ref, nmw_ref, win_ref, hnw_ref, lvw_ref, lvb_ref, wmix_ref, bcol_ref,
     wa_ref, wb_ref, wo_ref) = wrefs
    fdim = HEADS * LANES
    gw = GROUPS * LANES
    d_model = x.shape[1]
    offs = [0]
    for width in (fdim, fdim, fdim, fdim, gw, gw, d_model, d_model):
        offs.append(offs[-1] + width)

    xn = _rms(x, nmw_ref[...]).astype(BF16)

    def proj(i):
        return _dot(xn, win_ref[:, offs[i]:offs[i + 1]])

    lbl = lbl_ref[...]
    lbe = jnp.exp(lbl - jnp.max(lbl, axis=0, keepdims=True))
    lb = lbe[0:1, :] / jnp.sum(lbe, axis=0, keepdims=True)

    fg = lb + (1.0 - lb) * jax.nn.sigmoid(proj(1))
    k = 1.0 - fg
    log_f = jnp.log(fg)
    lf_hi = log_f.astype(BF16)
    lf_rem = log_f - lf_hi.astype(F32)
    lf_mid = lf_rem.astype(BF16)
    lf_lo = (lf_rem - lf_mid.astype(F32)).astype(BF16)
    lf3 = jnp.concatenate([lf_hi, lf_mid, lf_lo], axis=1)
    yield
    q = jax.nn.silu(proj(0))
    yield
    v_in = proj(2)
    yield
    r128 = lax.broadcasted_iota(jnp.int32, (LANES, LANES), 0)
    c128 = lax.broadcasted_iota(jnp.int32, (LANES, LANES), 1)
    shift = min(chunk, LANES).bit_length() - 1
    seg = jnp.where(((r128 >> shift) == (c128 >> shift)) & (c128 <= r128), 1.0, 0.0).astype(BF16)
    b_rows = []
    for g in range(rows // LANES):
        b3 = _dot(seg, lf3[g * LANES:(g + 1) * LANES, :])
        b_rows.append(b3[:, :fdim] + b3[:, fdim:2 * fdim] + b3[:, 2 * fdim:])
    b = jnp.concatenate(b_rows, axis=0)
    yield

    def head(hd):
        cols = slice(hd * LANES, (hd + 1) * LANES)
        o, st_out = yield from _hgrn_head(q[:, cols], k[:, cols], b[:, cols], v_in[:, cols],
                                          state_in(hd), chunk, scan)
        state_out(hd, st_out)
        return o

    o_heads = [(yield from head(0))]
    yield
    og = jax.nn.sigmoid(proj(3))
    yield
    o_heads.append((yield from head(1)))
    yield
    u = _gelu(proj(4))
    yield
    o_heads.append((yield from head(2)))
    yield
    gv = _gelu(proj(5))
    gc = gv - jnp.mean(gv, axis=-1, keepdims=True)
    v = gc * lax.rsqrt(jnp.mean(gc * gc, axis=-1, keepdims=True) + EPS) * lvw_ref[...] + lvb_ref[...]
    emit("v", v)
    yield
    o_heads.append((yield from head(3)))
    yield
    gate_a = jax.nn.sigmoid(proj(6))
    yield
    gate_b = jax.nn.sigmoid(proj(7))
    yield
    o_a = _rms(jnp.concatenate(o_heads, axis=1) * og, hnw_ref[...]).astype(BF16)

    vb = v.astype(BF16)
    if scan:
        wm = [jnp.where(c128 <= r128, wmix_ref[g], 0.0).astype(BF16) for g in range(GROUPS)]
        bcol = [bcol_ref[g] for g in range(GROUPS)]
    else:
        reps = LANES // chunk
        spread = jnp.where((r128 < chunk) & ((c128 & (chunk - 1)) == r128), 1.0, 0.0).astype(BF16)
        inside = ((r128 >> shift) == (c128 >> shift)) & (c128 <= r128)
        wm, bcol = [], []
        for g in range(GROUPS):
            corner = _dot(wmix_ref[g, :chunk, :].astype(BF16), spread)
            wm.append(jnp.where(inside, jnp.concatenate([corner] * reps, axis=0), 0.0).astype(BF16))
            bcol.append(jnp.concatenate([bcol_ref[g, :chunk, :]] * reps, axis=0))
    n_rg = rows // LANES
    mixed_g = [
        _dot(wm[g], jnp.concatenate(
            [vb[rg * LANES:(rg + 1) * LANES, g * LANES:(g + 1) * LANES] for rg in range(n_rg)],
            axis=1)) for g in range(GROUPS)]
    a_out = _dot(o_a, wa_ref[...])
    yield
    mixed = jnp.concatenate(
        [jnp.concatenate([mixed_g[g][:, rg * LANES:(rg + 1) * LANES] + bcol[g]
                          for g in range(GROUPS)], axis=1) for rg in range(n_rg)], axis=0)
    o_b = (u * mixed).astype(BF16)
    yield

    mix = (gate_a * a_out + gate_b * _dot(o_b, wb_ref[...])).astype(BF16)
    yield
    emit("h", x + _dot(mix, wo_ref[...]))


def _ffn_stream(h, p, wrefs, prev_rows, keep_rows, emit, *, rows, tpos, period=None):
    (nfw_ref, wup_ref, cw_ref, cb_ref, wdn_ref, npw_ref, wpg_ref, wpp_ref, fnw_ref) = wrefs
    d_ff = wdn_ref.shape[0]
    n_tiles = d_ff // FF_TILE
    hn = _rms(h, nfw_ref[...]).astype(BF16)

    def shift(a, k):
        if period is None:
            return pltpu.roll(a, k, 0)
        grouped = a.reshape(rows // period, period, a.shape[-1])
        return pltpu.roll(grouped, k, 1).reshape(a.shape)

    def conv(up, lo):
        cols = slice(lo, lo + FF_TILE)
        w = cw_ref[:, cols]
        e0, e1 = prev_rows(cols)
        keep_rows(cols, up)
        x1 = jnp.where(tpos >= 1, shift(up, 1), e1)
        x2 = jnp.where(tpos >= 2, shift(up, 2), jnp.where(tpos == 1, e1, e0))
        return cb_ref[:, cols] + x2 * w[0:1, :] + x1 * w[1:2, :] + up * w[2:3, :]

    def up_pair(j):
        lo_a = j * FF_TILE
        lo_b = d_ff + j * FF_TILE
        return (_dot(hn, wup_ref[:, lo_a:lo_a + FF_TILE]), _dot(hn, wup_ref[:, lo_b:lo_b + FF_TILE]))

    acc = jnp.zeros((rows, h.shape[1]), F32)
    nxt = up_pair(0)
    pp = _dot(p.astype(BF16), wpp_ref[...])
    gated = None
    yield
    for j in range(n_tiles + 1):
        lo_a = j * FF_TILE
        cur = nxt
        if j + 1 < n_tiles:
            nxt = up_pair(j + 1)
        if j < n_tiles:
            ya = conv(cur[0], lo_a)
            yb = conv(cur[1], d_ff + lo_a)
            new_gated = _gelu(ya.astype(BF16)) * yb.astype(BF16)
        if j > 0:
            acc = acc + _dot(gated, wdn_ref[lo_a - FF_TILE:lo_a, :])
        gated = new_gated
        yield
    h2 = h + acc
    gate = jax.nn.sigmoid(_dot(_rms(h2, npw_ref[...]).astype(BF16), wpg_ref[...]))
    yield
    h3 = h2 + gate * pp
    emit("y", _rms(h3, fnw_ref[...]))


def _prompt_kernel(*refs, rows, chunk, n_t, n_blocks):
    x_ref, p_ref = refs[:2]
    mixer_w = refs[2:13]
    ffn_w = refs[13:22]
    y_ref, sout_ref, cout_ref, st_s, carry_s, h_s = refs[22:]
    i = pl.program_id(0)
    t_mix = lax.rem(i, n_t)
    t_ffn = lax.rem(jnp.maximum(i - 1, 0), n_t)

    @pl.when(t_mix == 0)
    def _():
        st_s[...] = jnp.zeros_like(st_s)

    @pl.when(t_ffn == 0)
    def _():
        carry_s[...] = jnp.zeros_like(carry_s)

    slot = lax.rem(i, 2)

    def state_out(hd, states):
        st_s[hd] = states[-1]

    def prev_rows(cols):
        return carry_s[0:1, cols], carry_s[1:2, cols]

    def keep_rows(cols, up):
        carry_s[:, cols] = up[rows - 2:rows, :]
        cout_ref[0, :, cols] = up[rows - 2:rows, :]

    tpos = lax.broadcasted_iota(jnp.int32, (rows, FF_TILE), 0)

    def step(mixer, ffn):
        results = {}
        streams = []
        if mixer:
            streams.append(_mixer_stream(x_ref[...], mixer_w, lambda hd: st_s[hd], state_out,
                                         results.__setitem__, rows=rows, chunk=chunk, scan=True))
        if ffn:
            streams.append(_ffn_stream(h_s[1 - slot], p_ref[...], ffn_w, prev_rows, keep_rows,
                                       results.__setitem__, rows=rows, tpos=tpos))
        _interleave(streams, steps=PHASES if mixer and ffn else (1,), lead=1)
        if ffn:
            y_ref[...] = results["y"]
        if mixer:
            h_s[slot] = results["h"]

    pl.when(i == 0)(functools.partial(step, True, False))
    pl.when((i > 0) & (i < n_blocks))(functools.partial(step, True, True))
    pl.when(i == n_blocks)(functools.partial(step, False, True))

    @pl.when((t_mix == n_t - 1) & (i < n_blocks))
    def _():
        for hd in range(HEADS):
            sout_ref[0, hd] = st_s[hd].T


def _mixer_sample_kernel(*refs, rows, seq):
    x_ref = refs[0]
    wrefs = refs[1:12]
    sin_ref, h_ref, sout_ref, vout_ref = refs[12:]
    n_seq = rows // seq
    results = {}

    def state_out(hd, states):
        for j in range(n_seq):
            sout_ref[j, hd] = states[j].T

    for _ in _mixer_stream(x_ref[...], wrefs,
                           lambda hd: [sin_ref[j, hd].T for j in range(n_seq)], state_out,
                           results.__setitem__, rows=rows, chunk=seq, scan=False):
        pass
    vout_ref[...] = results["v"]
    h_ref[...] = results["h"]


def _ffn_sample_kernel(*refs, rows, seq):
    h_ref, p_ref = refs[:2]
    wrefs = refs[2:11]
    cst_ref, y_ref, cout_ref = refs[11:]
    n_seq = rows // seq
    results = {}

    def prev_rows(cols):
        st = cst_ref[:, :, cols]
        width = st.shape[-1]
        return tuple(jnp.broadcast_to(st[:, r:r + 1, :], (n_seq, seq, width)).reshape(rows, width)
                     for r in range(CONV_W - 1))

    def keep_rows(cols, up):
        cout_ref[:, :, cols] = up.reshape(n_seq, seq, up.shape[-1])[:, seq - (CONV_W - 1):seq, :]

    tpos = lax.broadcasted_iota(jnp.int32, (rows, FF_TILE), 0) & (seq - 1)
    for _ in _ffn_stream(h_ref[...], p_ref[...], wrefs, prev_rows, keep_rows, results.__setitem__,
                         rows=rows, tpos=tpos, period=seq):
        pass
    y_ref[...] = results["y"]


def _const_spec(shape):
    nd = len(shape)
    return pl.BlockSpec(shape, lambda *_: (0,) * nd, pipeline_mode=pl.Buffered(1))


def _params(n_axes):
    return pltpu.CompilerParams(dimension_semantics=("arbitrary",) * n_axes,
                                vmem_limit_bytes=VMEM_LIMIT)


def _cast_kernel(*refs):
    n = len(refs) // 2
    for src, dst in zip(refs[:n], refs[n:]):
        dst[...] = src[...].astype(dst.dtype)


def _to_bf16(mats):
    tile_rows = 2 * SUBLANES
    for m in mats:
        assert m.shape[0] % (CAST_STEPS * tile_rows) == 0 and m.shape[1] % LANES == 0
    specs = [pl.BlockSpec((m.shape[0] // CAST_STEPS, m.shape[1]), lambda i: (i, 0)) for m in mats]
    return pl.pallas_call(
        _cast_kernel, grid=(CAST_STEPS,), in_specs=specs, out_specs=specs,
        out_shape=[jax.ShapeDtypeStruct(m.shape, BF16) for m in mats],
        compiler_params=_params(1), name="weights_to_bf16",
    )(*mats)


def _prompt(x, p, mixer_consts, ffn_consts, *, rows, chunk):
    batch, seq, d = x.shape
    n_t = seq // rows
    n_blocks = batch * n_t
    width = ffn_consts[2].shape[1]
    mix_blk = lambda i: jnp.minimum(i, n_blocks - 1)
    ffn_blk = lambda i: jnp.maximum(i - 1, 0)
    consts = tuple(mixer_consts) + tuple(ffn_consts)
    in_specs = ([pl.BlockSpec((rows, d), lambda i: (mix_blk(i), 0)),
                 pl.BlockSpec((rows, p.shape[-1]), lambda i: (ffn_blk(i), 0))]
                + [_const_spec(a.shape) for a in consts])
    out_specs = [pl.BlockSpec((rows, d), lambda i: (ffn_blk(i), 0)),
                 pl.BlockSpec((1, HEADS, LANES, LANES), lambda i: (mix_blk(i) // n_t, 0, 0, 0)),
                 pl.BlockSpec((1, CONV_W - 1, width), lambda i: (ffn_blk(i) // n_t, 0, 0))]
    out_shape = [jax.ShapeDtypeStruct((batch * seq, d), F32),
                 jax.ShapeDtypeStruct((batch, HEADS, LANES, LANES), F32),
                 jax.ShapeDtypeStruct((batch, CONV_W - 1, width), F32)]
    scratch = [pltpu.VMEM((HEADS, LANES, LANES), F32), pltpu.VMEM((CONV_W - 1, width), F32),
               pltpu.VMEM((2, rows, d), F32)]
    y, st, cs = pl.pallas_call(
        functools.partial(_prompt_kernel, rows=rows, chunk=chunk, n_t=n_t, n_blocks=n_blocks),
        grid=(n_blocks + 1,), in_specs=in_specs, out_specs=out_specs, out_shape=out_shape,
        scratch_shapes=scratch, compiler_params=_params(1), name="prompt_step",
    )(x.reshape(batch * seq, d), p.reshape(batch * seq, -1), *consts)
    return y.reshape(batch, seq, d), st, cs


def _sample(x, p, state, conv_state, mixer_consts, ffn_consts, *, mixer_rows, ffn_rows):
    batch, seq, d = x.shape
    width = ffn_consts[2].shape[1]
    gw = GROUPS * LANES
    rows = mixer_rows
    n_seq = rows // seq
    row_map = lambda i: (i, 0)
    st_spec = pl.BlockSpec((n_seq, HEADS, LANES, LANES), lambda i: (i, 0, 0, 0))
    h, st, v = pl.pallas_call(
        functools.partial(_mixer_sample_kernel, rows=rows, seq=seq),
        grid=(batch // n_seq,),
        in_specs=([pl.BlockSpec((rows, d), row_map)] + [_const_spec(a.shape) for a in mixer_consts]
                  + [st_spec]),
        out_specs=[pl.BlockSpec((rows, d), row_map), st_spec, pl.BlockSpec((rows, gw), row_map)],
        out_shape=[jax.ShapeDtypeStruct((batch * seq, d), F32),
                   jax.ShapeDtypeStruct(state.shape, F32),
                   jax.ShapeDtypeStruct((batch * seq, gw), F32)],
        compiler_params=_params(1), name="mixer_sample",
    )(x.reshape(batch * seq, d), *mixer_consts, state)
    rows = ffn_rows
    n_seq = rows // seq
    cs_spec = pl.BlockSpec((n_seq, CONV_W - 1, width), lambda i: (i, 0, 0))
    y, cs = pl.pallas_call(
        functools.partial(_ffn_sample_kernel, rows=rows, seq=seq),
        grid=(batch // n_seq,),
        in_specs=([pl.BlockSpec((rows, d), row_map), pl.BlockSpec((rows, p.shape[-1]), row_map)]
                  + [_const_spec(a.shape) for a in ffn_consts] + [cs_spec]),
        out_specs=[pl.BlockSpec((rows, d), row_map), cs_spec],
        out_shape=[jax.ShapeDtypeStruct((batch * seq, d), F32),
                   jax.ShapeDtypeStruct(conv_state.shape, F32)],
        compiler_params=_params(1), name="ffn_sample",
    )(h, p.reshape(batch * seq, -1), *ffn_consts, conv_state)
    return y.reshape(batch, seq, d), st, cs, v.reshape(batch, seq, gw)


def kernel(x_prompt, x_sample, p_prompt, p_sample, state_hgrn, state_conv, lb_logits, norm_mix_w, w_in, hgrn_norm_w, ln_v_w, ln_v_b, w_spatial, b_spatial, w_a_out, w_b_out, w_o, norm_ffn_w, w_up, conv_w, conv_b, w_down, norm_ple_w, w_ple_gate, w_ple_proj, final_norm_w):
    depth = w_in.shape[0]
    assert depth == 1 and lb_logits.shape[0] == 2
    seq_s = x_sample.shape[1]
    assert seq_s == SUBLANES and LANES % seq_s == 0
    l = 0
    row = lambda a: a.reshape(1, -1)


    (w_in_b, w_a_b, w_b_b, w_o_b, w_up_b, w_down_b, w_pg_b, w_pp_b) = _to_bf16(
        [w_in[l], w_a_out[l], w_b_out[l], w_o[l], w_up[l], w_down[l], w_ple_gate[l], w_ple_proj[l]])

    mixer_consts = (lb_logits, row(norm_mix_w[l]), w_in_b, row(hgrn_norm_w[l]), row(ln_v_w[l]),
                    row(ln_v_b[l]), w_spatial[l], b_spatial[l][:, :, None], w_a_b, w_b_b, w_o_b)

    ffn_consts = (row(norm_ffn_w[l]), w_up_b, conv_w[l], row(conv_b[l]), w_down_b,
                  row(norm_ple_w[l]), w_pg_b, w_pp_b, row(final_norm_w))

    y_p, hs_p, cs_p = _prompt(x_prompt, p_prompt[l], mixer_consts, ffn_consts,
                              rows=BLOCK_ROWS, chunk=HGRN_CHUNK)
    y_s, hs_s, cs_s, v_s = _sample(x_sample, p_sample[l], state_hgrn[l], state_conv[l],
                                   mixer_consts, ffn_consts,
                                   mixer_rows=LANES, ffn_rows=BLOCK_ROWS)
    return (y_p, y_s, hs_p[None], hs_s[None], cs_p[None], cs_s[None], v_s[None])
```

```python
import functools

import jax
import jax.numpy as jnp
from jax import lax
from jax.experimental import pallas as pl
from jax.experimental.pallas import tpu as pltpu

F32 = jnp.float32
BF16 = jnp.bfloat16
EPS = 1e-6

HEADS = 4
GROUPS = 4
CONV_W = 3
LANES = 128
SUBLANES = 8
FF_TILE = 256
CAST_STEPS = 8
BLOCK_ROWS = 256
HGRN_CHUNK = 64
PHASES = (2, 1)
VMEM_LIMIT = 56 * 1024 * 1024

NT_DIMS = (((1,), (1,)), ((), ()))
TN_DIMS = (((0,), (0,)), ((), ()))


def _rms(x, w):
    return x * lax.rsqrt(jnp.mean(x * x, axis=-1, keepdims=True) + EPS) * w


def _gelu(x):
    return 0.5 * x * (1.0 + lax.erf(x * (2.0 ** -0.5)))


def _dot(a, b):
    return jnp.dot(a, b, preferred_element_type=F32)


def _hgrn_head(q, k, b, v, st_in, chunk, scan):
    rows = q.shape[0]
    n = rows // chunk
    shift = chunk.bit_length() - 1
    b3 = b.reshape(n, chunk, LANES)

    def chunk_row(r):
        return jnp.broadcast_to(b3[:, r:r + 1, :], b3.shape).reshape(rows, LANES)

    def blocked(a):
        cols = []
        for j in range(n):
            parts = [jnp.zeros((j * chunk, LANES), F32), a[j * chunk:(j + 1) * chunk],
                     jnp.zeros((rows - (j + 1) * chunk, LANES), F32)]
            cols.append(jnp.concatenate([p for p in parts if p.shape[0]], axis=0))
        return jnp.concatenate(cols, axis=1).astype(BF16)

    b_mid = chunk_row(chunk // 2 - 1)
    b_end = chunk_row(chunk - 1)
    q_t = (q * jnp.exp(b - b_mid)).astype(BF16)
    k_t = (k * jnp.exp(b_mid - b)).astype(BF16)
    vb = v.astype(BF16)
    k_blk = blocked(k * jnp.exp(b_end - b))
    q_blk = blocked(q * jnp.exp(b))
    yield
    scores = lax.dot_general(q_t, k_t, NT_DIMS, preferred_element_type=F32)
    ds = lax.dot_general(vb, k_blk, TN_DIMS, preferred_element_type=F32)
    yield
    row = lax.broadcasted_iota(jnp.int32, (rows, rows), 0)
    col = lax.broadcasted_iota(jnp.int32, (rows, rows), 1)
    keep = ((row >> shift) == (col >> shift)) & (col <= row)
    scores = jnp.where(keep, scores, 0.0).astype(BF16)
    before, after = [], []
    st = st_in
    for j in range(n):
        st_j = st if scan else st_in[j]
        before.append(st_j.astype(BF16))
        st = st_j * jnp.exp(b3[j, chunk - 1:chunk, :]) + ds[:, j * LANES:(j + 1) * LANES]
        after.append(st)
    s_rhs = jnp.concatenate(before, axis=1)
    yield
    o = _dot(scores, vb) + lax.dot_general(q_blk, s_rhs, NT_DIMS, preferred_element_type=F32)
    return o, after


def _interleave(streams, steps, lead=0):
    done = [False] * len(streams)
    for _ in range(lead):
        next(streams[0])
    while not all(done):
        for s, gen in enumerate(streams):
            for _ in range(steps[s]):
                if not done[s]:
                    try:
                        next(gen)
                    except StopIteration:
                        done[s] = True


def _mixer_stream(x, wrefs, state_in, state_out, emit, *, rows, chunk, scan):
    (lbl_ref, nmw_ref, win_ref, hnw_ref, lvw_ref, lvb_ref, wmix_ref, bcol_ref,
     wa_ref, wb_ref, wo_ref) = wrefs
    fdim = HEADS * LANES
    gw = GROUPS * LANES
    d_model = x.shape[1]
    offs = [0]
    for width in (fdim, fdim, fdim, fdim, gw, gw, d_model, d_model):
        offs.append(offs[-1] + width)

    xn = _rms(x, nmw_ref[...]).astype(BF16)

    def proj(i):
        return _dot(xn, win_ref[:, offs[i]:offs[i + 1]])

    lbl = lbl_ref[...]
    lbe = jnp.exp(lbl - jnp.max(lbl, axis=0, keepdims=True))
    lb = lbe[0:1, :] / jnp.sum(lbe, axis=0, keepdims=True)

    fg = lb + (1.0 - lb) * jax.nn.sigmoid(proj(1))
    k = 1.0 - fg
    log_f = jnp.log(fg)
    lf_hi = log_f.astype(BF16)
    lf_rem = log_f - lf_hi.astype(F32)
    lf_mid = lf_rem.astype(BF16)
    lf_lo = (lf_rem - lf_mid.astype(F32)).astype(BF16)
    lf3 = jnp.concatenate([lf_hi, lf_mid, lf_lo], axis=1)
    yield
    q = jax.nn.silu(proj(0))
    yield
    v_in = proj(2)
    yield
    r128 = lax.broadcasted_iota(jnp.int32, (LANES, LANES), 0)
    c128 = lax.broadcasted_iota(jnp.int32, (LANES, LANES), 1)
    shift = min(chunk, LANES).bit_length() - 1
    seg = jnp.where(((r128 >> shift) == (c128 >> shift)) & (c128 <= r128), 1.0, 0.0).astype(BF16)
    b_rows = []
    for g in range(rows // LANES):
        b3 = _dot(seg, lf3[g * LANES:(g + 1) * LANES, :])
        b_rows.append(b3[:, :fdim] + b3[:, fdim:2 * fdim] + b3[:, 2 * fdim:])
    b = jnp.concatenate(b_rows, axis=0)
    yield

    def head(hd):
        cols = slice(hd * LANES, (hd + 1) * LANES)
        o, st_out = yield from _hgrn_head(q[:, cols], k[:, cols], b[:, cols], v_in[:, cols],
                                          state_in(hd), chunk, scan)
        state_out(hd, st_out)
        return o

    o_heads = [(yield from head(0))]
    yield
    og = jax.nn.sigmoid(proj(3))
    yield
    o_heads.append((yield from head(1)))
    yield
    u = _gelu(proj(4))
    yield
    o_heads.append((yield from head(2)))
    yield
    gv = _gelu(proj(5))
    gc = gv - jnp.mean(gv, axis=-1, keepdims=True)
    v = gc * lax.rsqrt(jnp.mean(gc * gc, axis=-1, keepdims=True) + EPS) * lvw_ref[...] + lvb_ref[...]
    emit("v", v)
    yield
    o_heads.append((yield from head(3)))
    yield
    gate_a = jax.nn.sigmoid(proj(6))
    yield
    gate_b = jax.nn.sigmoid(proj(7))
    yield
    o_a = _rms(jnp.concatenate(o_heads, axis=1) * og, hnw_ref[...]).astype(BF16)

    vb = v.astype(BF16)
    if scan:
        wm = [jnp.where(c128 <= r128, wmix_ref[g], 0.0).astype(BF16) for g in range(GROUPS)]
        bcol = [bcol_ref[g] for g in range(GROUPS)]
    else:
        reps = LANES // chunk
        spread = jnp.where((r128 < chunk) & ((c128 & (chunk - 1)) == r128), 1.0, 0.0).astype(BF16)
        inside = ((r128 >> shift) == (c128 >> shift)) & (c128 <= r128)
        wm, bcol = [], []
        for g in range(GROUPS):
            corner = _dot(wmix_ref[g, :chunk, :].astype(BF16), spread)
            wm.append(jnp.where(inside, jnp.concatenate([corner] * reps, axis=0), 0.0).astype(BF16))
            bcol.append(jnp.concatenate([bcol_ref[g, :chunk, :]] * reps, axis=0))
    n_rg = rows // LANES
    mixed_g = [
        _dot(wm[g], jnp.concatenate(
            [vb[rg * LANES:(rg + 1) * LANES, g * LANES:(g + 1) * LANES] for rg in range(n_rg)],
            axis=1)) for g in range(GROUPS)]
    a_out = _dot(o_a, wa_ref[...])
    yield
    mixed = jnp.concatenate(
        [jnp.concatenate([mixed_g[g][:, rg * LANES:(rg + 1) * LANES] + bcol[g]
                          for g in range(GROUPS)], axis=1) for rg in range(n_rg)], axis=0)
    o_b = (u * mixed).astype(BF16)
    yield

    mix = (gate_a * a_out + gate_b * _dot(o_b, wb_ref[...])).astype(BF16)
    yield
    emit("h", x + _dot(mix, wo_ref[...]))


def _ffn_stream(h, p, wrefs, prev_rows, keep_rows, emit, *, rows, tpos, period=None):
    (nfw_ref, wup_ref, cw_ref, cb_ref, wdn_ref, npw_ref, wpg_ref, wpp_ref, fnw_ref) = wrefs
    d_ff = wdn_ref.shape[0]
    n_tiles = d_ff // FF_TILE
    hn = _rms(h, nfw_ref[...]).astype(BF16)

    def shift(a, k):
        if period is None:
            return pltpu.roll(a, k, 0)
        grouped = a.reshape(rows // period, period, a.shape[-1])
        return pltpu.roll(grouped, k, 1).reshape(a.shape)

    def conv(up, lo):
        cols = slice(lo, lo + FF_TILE)
        w = cw_ref[:, cols]
        e0, e1 = prev_rows(cols)
        keep_rows(cols, up)
        x1 = jnp.where(tpos >= 1, shift(up, 1), e1)
        x2 = jnp.where(tpos >= 1, shift(x1, 1), e0)
        return cb_ref[:, cols] + x2 * w[0:1, :] + x1 * w[1:2, :] + up * w[2:3, :]

    def up_pair(j):
        lo_a = j * FF_TILE
        lo_b = d_ff + j * FF_TILE
        return (_dot(hn, wup_ref[:, lo_a:lo_a + FF_TILE]), _dot(hn, wup_ref[:, lo_b:lo_b + FF_TILE]))

    acc = jnp.zeros((rows, h.shape[1]), F32)
    nxt = up_pair(0)
    pp = _dot(p.astype(BF16), wpp_ref[...])
    gated = None
    yield
    for j in range(n_tiles + 1):
        lo_a = j * FF_TILE
        cur = nxt
        if j + 1 < n_tiles:
            nxt = up_pair(j + 1)
        if j < n_tiles:
            ya = conv(cur[0], lo_a)
            yb = conv(cur[1], d_ff + lo_a)
            new_gated = _gelu(ya.astype(BF16)) * yb.astype(BF16)
        if j > 0:
            acc = acc + _dot(gated, wdn_ref[lo_a - FF_TILE:lo_a, :])
        gated = new_gated
        yield
    h2 = h + acc
    gate = jax.nn.sigmoid(_dot(_rms(h2, npw_ref[...]).astype(BF16), wpg_ref[...]))
    yield
    h3 = h2 + gate * pp
    emit("y", _rms(h3, fnw_ref[...]))


def _prompt_kernel(*refs, rows, chunk, n_t, n_blocks):
    x_ref, p_ref = refs[:2]
    mixer_w = refs[2:13]
    ffn_w = refs[13:22]
    y_ref, sout_ref, cout_ref, st_s, carry_s, h_s = refs[22:]
    i = pl.program_id(0)
    t_mix = lax.rem(i, n_t)
    t_ffn = lax.rem(jnp.maximum(i - 1, 0), n_t)

    @pl.when(t_mix == 0)
    def _():
        st_s[...] = jnp.zeros_like(st_s)

    @pl.when(t_ffn == 0)
    def _():
        carry_s[...] = jnp.zeros_like(carry_s)

    slot = lax.rem(i, 2)

    def state_out(hd, states):
        st_s[hd] = states[-1]

    def prev_rows(cols):
        return carry_s[0:1, cols], carry_s[1:2, cols]

    def keep_rows(cols, up):
        carry_s[:, cols] = up[rows - 2:rows, :]
        cout_ref[0, :, cols] = up[rows - 2:rows, :]

    tpos = lax.broadcasted_iota(jnp.int32, (rows, FF_TILE), 0)

    def step(mixer, ffn):
        results = {}
        streams = []
        if mixer:
            streams.append(_mixer_stream(x_ref[...], mixer_w, lambda hd: st_s[hd], state_out,
                                         results.__setitem__, rows=rows, chunk=chunk, scan=True))
        if ffn:
            streams.append(_ffn_stream(h_s[1 - slot], p_ref[...], ffn_w, prev_rows, keep_rows,
                                       results.__setitem__, rows=rows, tpos=tpos))
        _interleave(streams, steps=PHASES if mixer and ffn else (1,), lead=1)
        if ffn:
            y_ref[...] = results["y"]
        if mixer:
            h_s[slot] = results["h"]

    pl.when(i == 0)(functools.partial(step, True, False))
    pl.when((i > 0) & (i < n_blocks))(functools.partial(step, True, True))
    pl.when(i == n_blocks)(functools.partial(step, False, True))

    @pl.when((t_mix == n_t - 1) & (i < n_blocks))
    def _():
        for hd in range(HEADS):
            sout_ref[0, hd] = st_s[hd].T


def _mixer_sample_kernel(*refs, rows, seq):
    x_ref = refs[0]
    wrefs = refs[1:12]
    sin_ref, h_ref, sout_ref, vout_ref = refs[12:]
    n_seq = rows // seq
    results = {}

    def state_out(hd, states):
        for j in range(n_seq):
            sout_ref[j, hd] = states[j].T

    for _ in _mixer_stream(x_ref[...], wrefs,
                           lambda hd: [sin_ref[j, hd].T for j in range(n_seq)], state_out,
                           results.__setitem__, rows=rows, chunk=seq, scan=False):
        pass
    vout_ref[...] = results["v"]
    h_ref[...] = results["h"]


def _ffn_sample_kernel(*refs, rows, seq):
    h_ref, p_ref = refs[:2]
    wrefs = refs[2:11]
    cst_ref, y_ref, cout_ref = refs[11:]
    n_seq = rows // seq
    results = {}

    def prev_rows(cols):
        st = cst_ref[:, :, cols]
        width = st.shape[-1]
        return tuple(jnp.broadcast_to(st[:, r:r + 1, :], (n_seq, seq, width)).reshape(rows, width)
                     for r in range(CONV_W - 1))

    def keep_rows(cols, up):
        cout_ref[:, :, cols] = up.reshape(n_seq, seq, up.shape[-1])[:, seq - (CONV_W - 1):seq, :]

    tpos = lax.broadcasted_iota(jnp.int32, (rows, FF_TILE), 0) & (seq - 1)
    for _ in _ffn_stream(h_ref[...], p_ref[...], wrefs, prev_rows, keep_rows, results.__setitem__,
                         rows=rows, tpos=tpos, period=seq):
        pass
    y_ref[...] = results["y"]


def _const_spec(shape):
    nd = len(shape)
    return pl.BlockSpec(shape, lambda *_: (0,) * nd, pipeline_mode=pl.Buffered(1))


def _params(n_axes):
    return pltpu.CompilerParams(dimension_semantics=("arbitrary",) * n_axes,
                                vmem_limit_bytes=VMEM_LIMIT)


def _cast_kernel(*refs):
    n = len(refs) // 2
    for src, dst in zip(refs[:n], refs[n:]):
        dst[...] = src[...].astype(dst.dtype)


def _to_bf16(mats):
    tile_rows = 2 * SUBLANES
    for m in mats:
        assert m.shape[0] % (CAST_STEPS * tile_rows) == 0 and m.shape[1] % LANES == 0
    specs = [pl.BlockSpec((m.shape[0] // CAST_STEPS, m.shape[1]), lambda i: (i, 0)) for m in mats]
    return pl.pallas_call(
        _cast_kernel, grid=(CAST_STEPS,), in_specs=specs, out_specs=specs,
        out_shape=[jax.ShapeDtypeStruct(m.shape, BF16) for m in mats],
        compiler_params=_params(1), name="weights_to_bf16",
    )(*mats)


def _prompt(x, p, mixer_consts, ffn_consts, *, rows, chunk):
    batch, seq, d = x.shape
    n_t = seq // rows
    n_blocks = batch * n_t
    width = ffn_consts[2].shape[1]
    mix_blk = lambda i: jnp.minimum(i, n_blocks - 1)
    ffn_blk = lambda i: jnp.maximum(i - 1, 0)
    consts = tuple(mixer_consts) + tuple(ffn_consts)
    in_specs = ([pl.BlockSpec((rows, d), lambda i: (mix_blk(i), 0)),
                 pl.BlockSpec((rows, p.shape[-1]), lambda i: (ffn_blk(i), 0))]
                + [_const_spec(a.shape) for a in consts])
    out_specs = [pl.BlockSpec((rows, d), lambda i: (ffn_blk(i), 0)),
                 pl.BlockSpec((1, HEADS, LANES, LANES), lambda i: (mix_blk(i) // n_t, 0, 0, 0)),
                 pl.BlockSpec((1, CONV_W - 1, width), lambda i: (ffn_blk(i) // n_t, 0, 0))]
    out_shape = [jax.ShapeDtypeStruct((batch * seq, d), F32),
                 jax.ShapeDtypeStruct((batch, HEADS, LANES, LANES), F32),
                 jax.ShapeDtypeStruct((batch, CONV_W - 1, width), F32)]
    scratch = [pltpu.VMEM((HEADS, LANES, LANES), F32), pltpu.VMEM((CONV_W - 1, width), F32),
               pltpu.VMEM((2, rows, d), F32)]
    y, st, cs = pl.pallas_call(
        functools.partial(_prompt_kernel, rows=rows, chunk=chunk, n_t=n_t, n_blocks=n_blocks),
        grid=(n_blocks + 1,), in_specs=in_specs, out_specs=out_specs, out_shape=out_shape,
        scratch_shapes=scratch, compiler_params=_params(1), name="prompt_step",
    )(x.reshape(batch * seq, d), p.reshape(batch * seq, -1), *consts)
    return y.reshape(batch, seq, d), st, cs


def _sample(x, p, state, conv_state, mixer_consts, ffn_consts, *, mixer_rows, ffn_rows):
    batch, seq, d = x.shape
    width = ffn_consts[2].shape[1]
    gw = GROUPS * LANES
    rows = mixer_rows
    n_seq = rows // seq
    row_map = lambda i: (i, 0)
    st_spec = pl.BlockSpec((n_seq, HEADS, LANES, LANES), lambda i: (i, 0, 0, 0))
    h, st, v = pl.pallas_call(
        functools.partial(_mixer_sample_kernel, rows=rows, seq=seq),
        grid=(batch // n_seq,),
        in_specs=([pl.BlockSpec((rows, d), row_map)] + [_const_spec(a.shape) for a in mixer_consts]
                  + [st_spec]),
        out_specs=[pl.BlockSpec((rows, d), row_map), st_spec, pl.BlockSpec((rows, gw), row_map)],
        out_shape=[jax.ShapeDtypeStruct((batch * seq, d), F32),
                   jax.ShapeDtypeStruct(state.shape, F32),
                   jax.ShapeDtypeStruct((batch * seq, gw), F32)],
        compiler_params=_params(1), name="mixer_sample",
    )(x.reshape(batch * seq, d), *mixer_consts, state)
    rows = ffn_rows
    n_seq = rows // seq
    cs_spec = pl.BlockSpec((n_seq, CONV_W - 1, width), lambda i: (i, 0, 0))
    y, cs = pl.pallas_call(
        functools.partial(_ffn_sample_kernel, rows=rows, seq=seq),
        grid=(batch // n_seq,),
        in_specs=([pl.BlockSpec((rows, d), row_map), pl.BlockSpec((rows, p.shape[-1]), row_map)]
                  + [_const_spec(a.shape) for a in ffn_consts] + [cs_spec]),
        out_specs=[pl.BlockSpec((rows, d), row_map), cs_spec],
        out_shape=[jax.ShapeDtypeStruct((batch * seq, d), F32),
                   jax.ShapeDtypeStruct(conv_state.shape, F32)],
        compiler_params=_params(1), name="ffn_sample",
    )(h, p.reshape(batch * seq, -1), *ffn_consts, conv_state)
    return y.reshape(batch, seq, d), st, cs, v.reshape(batch, seq, gw)


def kernel(x_prompt, x_sample, p_prompt, p_sample, state_hgrn, state_conv, lb_logits, norm_mix_w, w_in, hgrn_norm_w, ln_v_w, ln_v_b, w_spatial, b_spatial, w_a_out, w_b_out, w_o, norm_ffn_w, w_up, conv_w, conv_b, w_down, norm_ple_w, w_ple_gate, w_ple_proj, final_norm_w):
    depth = w_in.shape[0]
    assert depth == 1 and lb_logits.shape[0] == 2
    seq_s = x_sample.shape[1]
    assert seq_s == SUBLANES and LANES % seq_s == 0
    l = 0
    row = lambda a: a.reshape(1, -1)


    (w_in_b, w_a_b, w_b_b, w_o_b, w_up_b, w_down_b, w_pg_b, w_pp_b) = _to_bf16(
        [w_in[l], w_a_out[l], w_b_out[l], w_o[l], w_up[l], w_down[l], w_ple_gate[l], w_ple_proj[l]])

    mixer_consts = (lb_logits, row(norm_mix_w[l]), w_in_b, row(hgrn_norm_w[l]), row(ln_v_w[l]),
                    row(ln_v_b[l]), w_spatial[l], b_spatial[l][:, :, None], w_a_b, w_b_b, w_o_b)

    ffn_consts = (row(norm_ffn_w[l]), w_up_b, conv_w[l], row(conv_b[l]), w_down_b,
                  row(norm_ple_w[l]), w_pg_b, w_pp_b, row(final_norm_w))

    y_p, hs_p, cs_p = _prompt(x_prompt, p_prompt[l], mixer_consts, ffn_consts,
                              rows=BLOCK_ROWS, chunk=HGRN_CHUNK)
    y_s, hs_s, cs_s, v_s = _sample(x_sample, p_sample[l], state_hgrn[l], state_conv[l],
                                   mixer_consts, ffn_consts,
                                   mixer_rows=LANES, ffn_rows=BLOCK_ROWS)
    return (y_p, y_s, hs_p[None], hs_s[None], cs_p[None], cs_s[None], v_s[None])
```

```python
import functools

import jax
import jax.numpy as jnp
from jax import lax
from jax.experimental import pallas as pl
from jax.experimental.pallas import tpu as pltpu

F32 = jnp.float32
BF16 = jnp.bfloat16
EPS = 1e-6

HEADS = 4
GROUPS = 4
CONV_W = 3
LANES = 128
SUBLANES = 8
FF_TILE = 256
CAST_STEPS = 8
BLOCK_ROWS = 256
HGRN_CHUNK = 64
PHASES = (2, 1)
VMEM_LIMIT = 56 * 1024 * 1024

NT_DIMS = (((1,), (1,)), ((), ()))
TN_DIMS = (((0,), (0,)), ((), ()))


def _rms(x, w):
    return x * lax.rsqrt(jnp.mean(x * x, axis=-1, keepdims=True) + EPS) * w


def _gelu(x):
    return 0.5 * x * (1.0 + lax.erf(x * (2.0 ** -0.5)))


def _dot(a, b):
    return jnp.dot(a, b, preferred_element_type=F32)


def _hgrn_head(q, k, b, v, st_in, chunk, scan):
    rows = q.shape[0]
    n = rows // chunk
    shift = chunk.bit_length() - 1
    b3 = b.reshape(n, chunk, LANES)

    def chunk_row(r):
        return jnp.broadcast_to(b3[:, r:r + 1, :], b3.shape).reshape(rows, LANES)

    def blocked(a):
        cols = []
        for j in range(n):
            parts = [jnp.zeros((j * chunk, LANES), F32), a[j * chunk:(j + 1) * chunk],
                     jnp.zeros((rows - (j + 1) * chunk, LANES), F32)]
            cols.append(jnp.concatenate([p for p in parts if p.shape[0]], axis=0))
        return jnp.concatenate(cols, axis=1).astype(BF16)

    b_mid = chunk_row(chunk // 2 - 1)
    b_end = chunk_row(chunk - 1)
    q_t = (q * jnp.exp(b - b_mid)).astype(BF16)
    k_t = (k * jnp.exp(b_mid - b)).astype(BF16)
    vb = v.astype(BF16)
    k_blk = blocked(k * jnp.exp(b_end - b))
    q_blk = blocked(q * jnp.exp(b))
    yield
    scores = lax.dot_general(q_t, k_t, NT_DIMS, preferred_element_type=F32)
    ds = lax.dot_general(vb, k_blk, TN_DIMS, preferred_element_type=F32)
    yield
    row = lax.broadcasted_iota(jnp.int32, (rows, rows), 0)
    col = lax.broadcasted_iota(jnp.int32, (rows, rows), 1)
    keep = ((row >> shift) == (col >> shift)) & (col <= row)
    scores = jnp.where(keep, scores, 0.0).astype(BF16)
    before, after = [], []
    st = st_in
    for j in range(n):
        st_j = st if scan else st_in[j]
        before.append(st_j.astype(BF16))
        st = st_j * jnp.exp(b3[j, chunk - 1:chunk, :]) + ds[:, j * LANES:(j + 1) * LANES]
        after.append(st)
    s_rhs = jnp.concatenate(before, axis=1)
    yield
    o = _dot(scores, vb) + lax.dot_general(q_blk, s_rhs, NT_DIMS, preferred_element_type=F32)
    return o, after


def _interleave(streams, steps, lead=0):
    done = [False] * len(streams)
    for _ in range(lead):
        next(streams[0])
    while not all(done):
        for s, gen in enumerate(streams):
            for _ in range(steps[s]):
                if not done[s]:
                    try:
                        next(gen)
                    except StopIteration:
                        done[s] = True


def _mixer_stream(x, wrefs, state_in, state_out, emit, *, rows, chunk, scan):
    (lbl_ref, nmw_ref, win_ref, hnw_ref, lvw_ref, lvb_ref, wmix_ref, bcol_ref,
     wa_ref, wb_ref, wo_ref) = wrefs
    fdim = HEADS * LANES
    gw = GROUPS * LANES
    d_model = x.shape[1]
    offs = [0]
    for width in (fdim, fdim, fdim, fdim, gw, gw, d_model, d_model):
        offs.append(offs[-1] + width)

    xn = _rms(x, nmw_ref[...]).astype(BF16)

    def proj(i):
        return _dot(xn, win_ref[:, offs[i]:offs[i + 1]])

    lbl = lbl_ref[...]
    lbe = jnp.exp(lbl - jnp.max(lbl, axis=0, keepdims=True))
    lb = lbe[0:1, :] / jnp.sum(lbe, axis=0, keepdims=True)

    fg = lb + (1.0 - lb) * jax.nn.sigmoid(proj(1))
    k = 1.0 - fg
    log_f = jnp.log(fg)
    lf_hi = log_f.astype(BF16)
    lf_rem = log_f - lf_hi.astype(F32)
    lf_mid = lf_rem.astype(BF16)
    lf_lo = (lf_rem - lf_mid.astype(F32)).astype(BF16)
    lf3 = jnp.concatenate([lf_hi, lf_mid, lf_lo], axis=1)
    yield
    q = jax.nn.silu(proj(0))
    yield
    v_in = proj(2)
    yield
    r128 = lax.broadcasted_iota(jnp.int32, (LANES, LANES), 0)
    c128 = lax.broadcasted_iota(jnp.int32, (LANES, LANES), 1)
    shift = min(chunk, LANES).bit_length() - 1
    seg = jnp.where(((r128 >> shift) == (c128 >> shift)) & (c128 <= r128), 1.0, 0.0).astype(BF16)
    b_rows = []
    for g in range(rows // LANES):
        b3 = _dot(seg, lf3[g * LANES:(g + 1) * LANES, :])
        b_rows.append(b3[:, :fdim] + b3[:, fdim:2 * fdim] + b3[:, 2 * fdim:])
    b = jnp.concatenate(b_rows, axis=0)
    yield

    def head(hd):
        cols = slice(hd * LANES, (hd + 1) * LANES)
        o, st_out = yield from _hgrn_head(q[:, cols], k[:, cols], b[:, cols], v_in[:, cols],
                                          state_in(hd), chunk, scan)
        state_out(hd, st_out)
        return o

    o_heads = [(yield from head(0))]
    yield
    og = jax.nn.sigmoid(proj(3))
    yield
    o_heads.append((yield from head(1)))
    yield
    u = _gelu(proj(4))
    yield
    o_heads.append((yield from head(2)))
    yield
    gv = _gelu(proj(5))
    gc = gv - jnp.mean(gv, axis=-1, keepdims=True)
    v = gc * lax.rsqrt(jnp.mean(gc * gc, axis=-1, keepdims=True) + EPS) * lvw_ref[...] + lvb_ref[...]
    emit("v", v)
    yield
    o_heads.append((yield from head(3)))
    yield
    gate_a = jax.nn.sigmoid(proj(6))
    yield
    gate_b = jax.nn.sigmoid(proj(7))
    yield
    o_a = _rms(jnp.concatenate(o_heads, axis=1) * og, hnw_ref[...]).astype(BF16)

    vb = v.astype(BF16)
    if scan:
        wm = [jnp.where(c128 <= r128, wmix_ref[g], 0.0).astype(BF16) for g in range(GROUPS)]
        bcol = [bcol_ref[g] for g in range(GROUPS)]
    else:
        reps = LANES // chunk
        spread = jnp.where((r128 < chunk) & ((c128 & (chunk - 1)) == r128), 1.0, 0.0).astype(BF16)
        inside = ((r128 >> shift) == (c128 >> shift)) & (c128 <= r128)
        wm, bcol = [], []
        for g in range(GROUPS):
            corner = _dot(wmix_ref[g, :chunk, :].astype(BF16), spread)
            wm.append(jnp.where(inside, jnp.concatenate([corner] * reps, axis=0), 0.0).astype(BF16))
            bcol.append(jnp.concatenate([bcol_ref[g, :chunk, :]] * reps, axis=0))
    n_rg = rows // LANES
    mixed_g = [
        _dot(wm[g], jnp.concatenate(
            [vb[rg * LANES:(rg + 1) * LANES, g * LANES:(g + 1) * LANES] for rg in range(n_rg)],
            axis=1)) for g in range(GROUPS)]
    a_out = _dot(o_a, wa_ref[...])
    yield
    mixed = jnp.concatenate(
        [jnp.concatenate([mixed_g[g][:, rg * LANES:(rg + 1) * LANES] + bcol[g]
                          for g in range(GROUPS)], axis=1) for rg in range(n_rg)], axis=0)
    o_b = (u * mixed).astype(BF16)
    yield

    mix = (gate_a * a_out + gate_b * _dot(o_b, wb_ref[...])).astype(BF16)
    yield
    emit("h", x + _dot(mix, wo_ref[...]))


def _ffn_stream(h, p, wrefs, prev_rows, keep_rows, emit, *, rows, tpos, period=None):
    (nfw_ref, wup_ref, cw_ref, cb_ref, wdn_ref, npw_ref, wpg_ref, wpp_ref, fnw_ref) = wrefs
    d_ff = wdn_ref.shape[0]
    n_tiles = d_ff // FF_TILE
    hn = _rms(h, nfw_ref[...]).astype(BF16)

    def shift(a, k):
        if period is None:
            return pltpu.roll(a, k, 0)
        grouped = a.reshape(rows // period, period, a.shape[-1])
        return pltpu.roll(grouped, k, 1).reshape(a.shape)

    def conv(up, lo):
        cols = slice(lo, lo + up.shape[-1])
        w = cw_ref[:, cols]
        e0, e1 = prev_rows(cols)
        keep_rows(cols, up)
        first = tpos[:, :up.shape[-1]] >= 1
        x1 = jnp.where(first, shift(up, 1), e1)
        x2 = jnp.where(first, shift(x1, 1), e0)
        return cb_ref[:, cols] + x2 * w[0:1, :] + x1 * w[1:2, :] + up * w[2:3, :]

    def gate(up_a, up_b, lo):
        strips = []
        for c0 in range(0, FF_TILE, FF_TILE // 2):
            strip = slice(c0, c0 + FF_TILE // 2)
            ya = conv(up_a[:, strip], lo + c0)
            yb = conv(up_b[:, strip], d_ff + lo + c0)
            strips.append(_gelu(ya.astype(BF16)) * yb.astype(BF16))
        return jnp.concatenate(strips, axis=1)

    def up_pair(j):
        lo_a = j * FF_TILE
        lo_b = d_ff + j * FF_TILE
        return (_dot(hn, wup_ref[:, lo_a:lo_a + FF_TILE]), _dot(hn, wup_ref[:, lo_b:lo_b + FF_TILE]))

    acc = jnp.zeros((rows, h.shape[1]), F32)
    nxt = up_pair(0)
    pp = _dot(p.astype(BF16), wpp_ref[...])
    gated = None
    yield
    for j in range(n_tiles + 1):
        lo_a = j * FF_TILE
        cur = nxt
        if j + 1 < n_tiles:
            nxt = up_pair(j + 1)
        if j < n_tiles:
            new_gated = gate(cur[0], cur[1], lo_a)
        if j > 0:
            acc = acc + _dot(gated, wdn_ref[lo_a - FF_TILE:lo_a, :])
        gated = new_gated
        yield
    h2 = h + acc
    gate = jax.nn.sigmoid(_dot(_rms(h2, npw_ref[...]).astype(BF16), wpg_ref[...]))
    yield
    h3 = h2 + gate * pp
    emit("y", _rms(h3, fnw_ref[...]))


def _prompt_kernel(*refs, rows, chunk, n_t, n_blocks):
    x_ref, p_ref = refs[:2]
    mixer_w = refs[2:13]
    ffn_w = refs[13:22]
    y_ref, sout_ref, cout_ref, st_s, carry_s, h_s = refs[22:]
    i = pl.program_id(0)
    t_mix = lax.rem(i, n_t)
    t_ffn = lax.rem(jnp.maximum(i - 1, 0), n_t)

    @pl.when(t_mix == 0)
    def _():
        st_s[...] = jnp.zeros_like(st_s)

    @pl.when(t_ffn == 0)
    def _():
        carry_s[...] = jnp.zeros_like(carry_s)

    slot = lax.rem(i, 2)

    def state_out(hd, states):
        st_s[hd] = states[-1]

    def prev_rows(cols):
        return carry_s[0:1, cols], carry_s[1:2, cols]

    def keep_rows(cols, up):
        carry_s[:, cols] = up[rows - 2:rows, :]
        cout_ref[0, :, cols] = up[rows - 2:rows, :]

    tpos = lax.broadcasted_iota(jnp.int32, (rows, FF_TILE), 0)

    def step(mixer, ffn):
        results = {}
        streams = []
        if mixer:
            streams.append(_mixer_stream(x_ref[...], mixer_w, lambda hd: st_s[hd], state_out,
                                         results.__setitem__, rows=rows, chunk=chunk, scan=True))
        if ffn:
            streams.append(_ffn_stream(h_s[1 - slot], p_ref[...], ffn_w, prev_rows, keep_rows,
                                       results.__setitem__, rows=rows, tpos=tpos))
        _interleave(streams, steps=PHASES if mixer and ffn else (1,), lead=1)
        if ffn:
            y_ref[...] = results["y"]
        if mixer:
            h_s[slot] = results["h"]

    pl.when(i == 0)(functools.partial(step, True, False))
    pl.when((i > 0) & (i < n_blocks))(functools.partial(step, True, True))
    pl.when(i == n_blocks)(functools.partial(step, False, True))

    @pl.when((t_mix == n_t - 1) & (i < n_blocks))
    def _():
        for hd in range(HEADS):
            sout_ref[0, hd] = st_s[hd].T


def _mixer_sample_kernel(*refs, rows, seq):
    x_ref = refs[0]
    wrefs = refs[1:12]
    sin_ref, h_ref, sout_ref, vout_ref = refs[12:]
    n_seq = rows // seq
    results = {}

    def state_out(hd, states):
        for j in range(n_seq):
            sout_ref[j, hd] = states[j].T

    for _ in _mixer_stream(x_ref[...], wrefs,
                           lambda hd: [sin_ref[j, hd].T for j in range(n_seq)], state_out,
                           results.__setitem__, rows=rows, chunk=seq, scan=False):
        pass
    vout_ref[...] = results["v"]
    h_ref[...] = results["h"]


def _ffn_sample_kernel(*refs, rows, seq):
    h_ref, p_ref = refs[:2]
    wrefs = refs[2:11]
    cst_ref, y_ref, cout_ref = refs[11:]
    n_seq = rows // seq
    results = {}

    def prev_rows(cols):
        st = cst_ref[:, :, cols]
        width = st.shape[-1]
        return tuple(jnp.broadcast_to(st[:, r:r + 1, :], (n_seq, seq, width)).reshape(rows, width)
                     for r in range(CONV_W - 1))

    def keep_rows(cols, up):
        cout_ref[:, :, cols] = up.reshape(n_seq, seq, up.shape[-1])[:, seq - (CONV_W - 1):seq, :]

    tpos = lax.broadcasted_iota(jnp.int32, (rows, FF_TILE), 0) & (seq - 1)
    for _ in _ffn_stream(h_ref[...], p_ref[...], wrefs, prev_rows, keep_rows, results.__setitem__,
                         rows=rows, tpos=tpos, period=seq):
        pass
    y_ref[...] = results["y"]


def _const_spec(shape):
    nd = len(shape)
    return pl.BlockSpec(shape, lambda *_: (0,) * nd, pipeline_mode=pl.Buffered(1))


def _params(n_axes):
    return pltpu.CompilerParams(dimension_semantics=("arbitrary",) * n_axes,
                                vmem_limit_bytes=VMEM_LIMIT)


def _cast_kernel(*refs):
    n = len(refs) // 2
    for src, dst in zip(refs[:n], refs[n:]):
        dst[...] = src[...].astype(dst.dtype)


def _to_bf16(mats):
    tile_rows = 2 * SUBLANES
    for m in mats:
        assert m.shape[0] % (CAST_STEPS * tile_rows) == 0 and m.shape[1] % LANES == 0
    specs = [pl.BlockSpec((m.shape[0] // CAST_STEPS, m.shape[1]), lambda i: (i, 0)) for m in mats]
    return pl.pallas_call(
        _cast_kernel, grid=(CAST_STEPS,), in_specs=specs, out_specs=specs,
        out_shape=[jax.ShapeDtypeStruct(m.shape, BF16) for m in mats],
        compiler_params=_params(1), name="weights_to_bf16",
    )(*mats)


def _prompt(x, p, mixer_consts, ffn_consts, *, rows, chunk):
    batch, seq, d = x.shape
    n_t = seq // rows
    n_blocks = batch * n_t
    width = ffn_consts[2].shape[1]
    mix_blk = lambda i: jnp.minimum(i, n_blocks - 1)
    ffn_blk = lambda i: jnp.maximum(i - 1, 0)
    consts = tuple(mixer_consts) + tuple(ffn_consts)
    in_specs = ([pl.BlockSpec((rows, d), lambda i: (mix_blk(i), 0)),
                 pl.BlockSpec((rows, p.shape[-1]), lambda i: (ffn_blk(i), 0))]
                + [_const_spec(a.shape) for a in consts])
    out_specs = [pl.BlockSpec((rows, d), lambda i: (ffn_blk(i), 0)),
                 pl.BlockSpec((1, HEADS, LANES, LANES), lambda i: (mix_blk(i) // n_t, 0, 0, 0)),
                 pl.BlockSpec((1, CONV_W - 1, width), lambda i: (ffn_blk(i) // n_t, 0, 0))]
    out_shape = [jax.ShapeDtypeStruct((batch * seq, d), F32),
                 jax.ShapeDtypeStruct((batch, HEADS, LANES, LANES), F32),
                 jax.ShapeDtypeStruct((batch, CONV_W - 1, width), F32)]
    scratch = [pltpu.VMEM((HEADS, LANES, LANES), F32), pltpu.VMEM((CONV_W - 1, width), F32),
               pltpu.VMEM((2, rows, d), F32)]
    y, st, cs = pl.pallas_call(
        functools.partial(_prompt_kernel, rows=rows, chunk=chunk, n_t=n_t, n_blocks=n_blocks),
        grid=(n_blocks + 1,), in_specs=in_specs, out_specs=out_specs, out_shape=out_shape,
        scratch_shapes=scratch, compiler_params=_params(1), name="prompt_step",
    )(x.reshape(batch * seq, d), p.reshape(batch * seq, -1), *consts)
    return y.reshape(batch, seq, d), st, cs


def _sample(x, p, state, conv_state, mixer_consts, ffn_consts, *, mixer_rows, ffn_rows):
    batch, seq, d = x.shape
    width = ffn_consts[2].shape[1]
    gw = GROUPS * LANES
    rows = mixer_rows
    n_seq = rows // seq
    row_map = lambda i: (i, 0)
    st_spec = pl.BlockSpec((n_seq, HEADS, LANES, LANES), lambda i: (i, 0, 0, 0))
    h, st, v = pl.pallas_call(
        functools.partial(_mixer_sample_kernel, rows=rows, seq=seq),
        grid=(batch // n_seq,),
        in_specs=([pl.BlockSpec((rows, d), row_map)] + [_const_spec(a.shape) for a in mixer_consts]
                  + [st_spec]),
        out_specs=[pl.BlockSpec((rows, d), row_map), st_spec, pl.BlockSpec((rows, gw), row_map)],
        out_shape=[jax.ShapeDtypeStruct((batch * seq, d), F32),
                   jax.ShapeDtypeStruct(state.shape, F32),
                   jax.ShapeDtypeStruct((batch * seq, gw), F32)],
        compiler_params=_params(1), name="mixer_sample",
    )(x.reshape(batch * seq, d), *mixer_consts, state)
    rows = ffn_rows
    n_seq = rows // seq
    cs_spec = pl.BlockSpec((n_seq, CONV_W - 1, width), lambda i: (i, 0, 0))
    y, cs = pl.pallas_call(
        functools.partial(_ffn_sample_kernel, rows=rows, seq=seq),
        grid=(batch // n_seq,),
        in_specs=([pl.BlockSpec((rows, d), row_map), pl.BlockSpec((rows, p.shape[-1]), row_map)]
                  + [_const_spec(a.shape) for a in ffn_consts] + [cs_spec]),
        out_specs=[pl.BlockSpec((rows, d), row_map), cs_spec],
        out_shape=[jax.ShapeDtypeStruct((batch * seq, d), F32),
                   jax.ShapeDtypeStruct(conv_state.shape, F32)],
        compiler_params=_params(1), name="ffn_sample",
    )(h, p.reshape(batch * seq, -1), *ffn_consts, conv_state)
    return y.reshape(batch, seq, d), st, cs, v.reshape(batch, seq, gw)


def kernel(x_prompt, x_sample, p_prompt, p_sample, state_hgrn, state_conv, lb_logits, norm_mix_w, w_in, hgrn_norm_w, ln_v_w, ln_v_b, w_spatial, b_spatial, w_a_out, w_b_out, w_o, norm_ffn_w, w_up, conv_w, conv_b, w_down, norm_ple_w, w_ple_gate, w_ple_proj, final_norm_w):
    depth = w_in.shape[0]
    assert depth == 1 and lb_logits.shape[0] == 2
    seq_s = x_sample.shape[1]
    assert seq_s == SUBLANES and LANES % seq_s == 0
    l = 0
    row = lambda a: a.reshape(1, -1)


    (w_in_b, w_a_b, w_b_b, w_o_b, w_up_b, w_down_b, w_pg_b, w_pp_b) = _to_bf16(
        [w_in[l], w_a_out[l], w_b_out[l], w_o[l], w_up[l], w_down[l], w_ple_gate[l], w_ple_proj[l]])

    mixer_consts = (lb_logits, row(norm_mix_w[l]), w_in_b, row(hgrn_norm_w[l]), row(ln_v_w[l]),
                    row(ln_v_b[l]), w_spatial[l], b_spatial[l][:, :, None], w_a_b, w_b_b, w_o_b)

    ffn_consts = (row(norm_ffn_w[l]), w_up_b, conv_w[l], row(conv_b[l]), w_down_b,
                  row(norm_ple_w[l]), w_pg_b, w_pp_b, row(final_norm_w))

    y_p, hs_p, cs_p = _prompt(x_prompt, p_prompt[l], mixer_consts, ffn_consts,
                              rows=BLOCK_ROWS, chunk=HGRN_CHUNK)
    y_s, hs_s, cs_s, v_s = _sample(x_sample, p_sample[l], state_hgrn[l], state_conv[l],
                                   mixer_consts, ffn_consts,
                                   mixer_rows=LANES, ffn_rows=BLOCK_ROWS)
    return (y_p, y_s, hs_p[None], hs_s[None], cs_p[None], cs_s[None], v_s[None])
```
